```python
import math
import jax, jax.numpy as jnp
from jax import lax
import numpy as np

D_MODEL = 1024
BATCH = 16
SEQ = 2048
DEPTH = 4
DEC_BATCH = 128
DEC_SEQ = 1
PAST_LEN = 8192
PAGE_SIZE = 128

N_MIXERS = 3
HEAD_DIM = 64
ROT_DIM = HEAD_DIM // 4
ROPE_THETA = 500000.0
NORM_EPS = 1e-6
D_FF = 4 * D_MODEL
QBLK = 128
ATTN_SCALE = HEAD_DIM ** -0.5

NSA_HEADS = D_MODEL // HEAD_DIM
NSA_KV = 2
NSA_HPG = NSA_HEADS // NSA_KV
CMP_BLOCK = 32
CMP_STRIDE = 16
SLC_BLOCK = 64
SLC_TOPK = 16
NSA_WINDOW = 512
SLC_QBLK = 32
NSA_IN = NSA_HEADS * HEAD_DIM + 6 * NSA_KV * HEAD_DIM + 3 * NSA_HEADS

DIL_PAIRS = ((128, 1), (512, 4), (2048, 16))
DIL_HEADS = 8
DIL_BLK = 128
DIL_IN = len(DIL_PAIRS) * 3 * DIL_HEADS * HEAD_DIM

MLA_HEADS = 16
MLA_NOPE = 64
MLA_ROPE = 32
MLA_V = 64
MLA_KV_RANK = 256
MLA_Q_RANK = 384
MLA_THETA = 10000.0
MLA_SCALE = (MLA_NOPE + MLA_ROPE) ** -0.5
MLA_IN = MLA_Q_RANK + MLA_KV_RANK + MLA_ROPE

kernel_name = 'hybrid_nsa_dilated_mla_decode_step'


def rms_norm(x, g):
    xf = x.astype(jnp.float32)
    y = xf * lax.rsqrt(jnp.mean(xf * xf, axis=-1, keepdims=True) + NORM_EPS)
    return (y * g.astype(jnp.float32)).astype(x.dtype)


def rope(x, pos, rot, theta):
    half = rot // 2
    inv = theta ** (-(jnp.arange(half, dtype=jnp.float32) * 2.0) / rot)
    ang = pos.astype(jnp.float32)[:, None] * inv[None, :]
    shape = (pos.shape[0],) + (1,) * (x.ndim - 3) + (half,)
    cos = jnp.cos(ang).reshape(shape)
    sin = jnp.sin(ang).reshape(shape)
    xr = x[..., :rot].astype(jnp.float32)
    x1, x2 = xr[..., :half], xr[..., half:]
    out = jnp.concatenate([x1 * cos - x2 * sin, x2 * cos + x1 * sin], axis=-1).astype(x.dtype)
    return jnp.concatenate([out, x[..., rot:]], axis=-1)


def masked_softmax(s, mask):
    s = jnp.where(mask, s.astype(jnp.float32), -jnp.inf)
    m = jnp.max(s, axis=-1, keepdims=True)
    m = jnp.where(jnp.isfinite(m), m, 0.0)
    e = jnp.exp(s - m)
    l = jnp.sum(e, axis=-1, keepdims=True)
    p = e / jnp.maximum(l, 1e-30)
    lse = m[..., 0] + jnp.log(jnp.maximum(l[..., 0], 1e-30))
    return p, lse


def sqrelu_mlp(x, w_up, w_down):
    h = jax.nn.relu(x @ w_up)
    return (h * h) @ w_down


def tail_concat(buf, new, keep):
    start = buf.shape[1] + new.shape[1] - keep
    if start >= buf.shape[1]:
        return new[:, start - buf.shape[1]:]
    return jnp.concatenate([buf[:, start:], new], axis=1)


def nsa_project(x, w_in, pos):
    B, T, _ = x.shape
    nq = NSA_HEADS * HEAD_DIM
    nkv = 6 * NSA_KV * HEAD_DIM
    h = x @ w_in
    q = rope(h[..., :nq].reshape(B, T, NSA_HEADS, HEAD_DIM), pos, ROT_DIM, ROPE_THETA)
    kv = h[..., nq:nq + nkv].reshape(B, T, 3, 2, NSA_KV, HEAD_DIM)
    k = rope(kv[:, :, :, 0], pos, ROT_DIM, ROPE_THETA)
    kv = jnp.stack([k, kv[:, :, :, 1]], axis=3).reshape(B, T, 6, NSA_KV, HEAD_DIM)
    gates = jax.nn.sigmoid(h[..., nq + nkv:].reshape(B, T, 3, NSA_KV, NSA_HPG, 1))
    return q.reshape(B, T, NSA_KV, NSA_HPG, HEAD_DIM), kv, gates


def cmp_chunk_sums(rows, pos_w):
    B, L = rows.shape[:2]
    c = rows.reshape(B, L // CMP_STRIDE, CMP_STRIDE, 2, NSA_KV, HEAD_DIM)
    first = jnp.einsum('bnakgd,kad->bnkgd', c, pos_w[:, :CMP_STRIDE])
    second = jnp.einsum('bnakgd,kad->bnkgd', c, pos_w[:, CMP_STRIDE:])
    return first, second


def cmp_blocks(first, second, proj):
    s = first[:, :-1] + second[:, 1:]
    return jnp.einsum('bnkgd,kde->bnkge', s, proj)


def nsa_compressed(q, q_pos, ckv):
    n_cmp = ckv.shape[1]
    s = jnp.einsum('bqghd,bngd->bqghn', q, ckv[:, :, 0]) * ATTN_SCALE
    blk_end = jnp.arange(n_cmp) * CMP_STRIDE + CMP_BLOCK - 1
    mask = (blk_end[None, :] <= q_pos[:, None])[None, :, None, None, :]
    p, _ = masked_softmax(s, mask)
    o = jnp.einsum('bqghn,bngd->bqghd', p.astype(ckv.dtype), ckv[:, :, 1])
    return o, p


def nsa_select(p_cmp, q_pos, n_sel):
    ratio = SLC_BLOCK // CMP_STRIDE
    imp = jnp.sum(p_cmp, axis=3)
    imp = jnp.pad(imp, ((0, 0), (0, 0), (0, 0), (0, n_sel * ratio - imp.shape[-1])))
    imp = imp.reshape(imp.shape[:3] + (n_sel, ratio))
    prev = jnp.pad(imp[..., :-1, -1], ((0, 0), (0, 0), (0, 0), (1, 0)))
    score = jnp.sum(imp, axis=-1) + prev
    blk = jnp.arange(n_sel)[None, :]
    cur = (q_pos // SLC_BLOCK)[:, None]
    forced = (blk == 0) | (blk == cur) | (blk == cur - 1)
    future = blk * SLC_BLOCK > q_pos[:, None]
    score = jnp.where(forced[None, :, None, :], jnp.inf,
                      jnp.where(future[None, :, None, :], -jnp.inf, score))
    _, idx = lax.top_k(score, min(SLC_TOPK, n_sel))
    return idx


def gather_blocks(src, blk_idx):
    bi = jnp.arange(src.shape[0])[:, None, None, None]
    gi = jnp.arange(NSA_KV)[None, None, :, None]
    return src[bi, blk_idx, :, :, gi, :]


def nsa_selected(q, q_pos, sel_kv, idx):
    B, Q, G, H, _ = q.shape
    n = idx.shape[-1]
    s = jnp.einsum('bqghd,bqgnkd->bqghnk', q, sel_kv[..., 0, :]) * ATTN_SCALE
    kpos = idx[..., None] * SLC_BLOCK + jnp.arange(SLC_BLOCK)
    mask = (kpos <= q_pos[None, :, None, None, None]).reshape(B, Q, G, 1, n * SLC_BLOCK)
    p, _ = masked_softmax(s.reshape(B, Q, G, H, n * SLC_BLOCK), mask)
    v = sel_kv[..., 1, :].reshape(B, Q, G, n * SLC_BLOCK, HEAD_DIM)
    return jnp.einsum('bqghm,bqgmd->bqghd', p.astype(v.dtype), v)


def nsa_window(q, q_pos, wkv, k_pos):
    dist = q_pos[:, None] - k_pos[None, :]
    mask = ((dist >= 0) & (dist <= NSA_WINDOW) & (k_pos[None, :] >= 0))[None, :, None, None, :]
    s = jnp.einsum('bqghd,bkgd->bqghk', q, wkv[:, :, 0]) * ATTN_SCALE
    p, _ = masked_softmax(s, mask)
    return jnp.einsum('bqghk,bkgd->bqghd', p.astype(wkv.dtype), wkv[:, :, 1])


def nsa_output(o_c, o_s, o_w, gates, w_out):
    o = gates[:, :, 0] * o_c + gates[:, :, 1] * o_s + gates[:, :, 2] * o_w
    B, T = o.shape[:2]
    return o.reshape(B, T, NSA_HEADS * HEAD_DIM) @ w_out


def nsa_prompt(x, p):
    B, T, _ = x.shape
    pos = jnp.arange(T)
    q, kv, gates = nsa_project(x, p['w_in'], pos)
    first, second = cmp_chunk_sums(kv[:, :, 0:2], p['cmp_pos'])
    ckv = cmp_blocks(first, second, p['cmp_proj'])
    o_c, p_cmp = nsa_compressed(q, pos, ckv)
    n_sel = T // SLC_BLOCK
    idx = nsa_select(p_cmp, pos, n_sel)
    slc = kv[:, :, 2:4].reshape(B, n_sel, SLC_BLOCK, 2, NSA_KV, HEAD_DIM)
    nqc = T // SLC_QBLK
    q_c = q.reshape((B, nqc, SLC_QBLK) + q.shape[2:]).swapaxes(0, 1)
    i_c = idx.reshape((B, nqc, SLC_QBLK) + idx.shape[2:]).swapaxes(0, 1)
    pos_c = pos.reshape(nqc, SLC_QBLK)

    def sel_block(args):
        qb, ib, pb = args
        return nsa_selected(qb, pb, gather_blocks(slc, ib), ib)

    o_s = lax.map(sel_block, (q_c, i_c, pos_c)).swapaxes(0, 1).reshape(q.shape)
    nb = T // QBLK
    span = NSA_WINDOW + QBLK
    wkv = jnp.pad(kv[:, :, 4:6], ((0, 0), (NSA_WINDOW, 0), (0, 0), (0, 0), (0, 0)))
    q_b = q.reshape((B, nb, QBLK) + q.shape[2:]).swapaxes(0, 1)

    def win_block(args):
        n, qb = args
        kvb = lax.dynamic_slice_in_dim(wkv, n * QBLK, span, axis=1)
        return nsa_window(qb, n * QBLK + jnp.arange(QBLK), kvb, n * QBLK - NSA_WINDOW + jnp.arange(span))

    o_w = lax.map(win_block, (jnp.arange(nb), q_b)).swapaxes(0, 1).reshape(q.shape)
    y = nsa_output(o_c, o_s, o_w, gates, p['w_out'])
    return y, kv[:, :, 0:4], kv[:, T - min(NSA_WINDOW, T):, 4:6]


def nsa_sample(x, p, pool, win_buf, page_table):
    B, S, _ = x.shape
    P = page_table.shape[1] * PAGE_SIZE
    pos = P + jnp.arange(S)
    q, kv, gates = nsa_project(x, p['w_in'], pos)
    s_pad = -(-S // SLC_BLOCK) * SLC_BLOCK
    new4 = jnp.pad(kv[:, :, 0:4], ((0, 0), (0, s_pad - S), (0, 0), (0, 0), (0, 0)))
    past_cmp = pool[page_table, :, 0:2].reshape(B, P, 2, NSA_KV, HEAD_DIM)
    pf, ps = cmp_chunk_sums(past_cmp, p['cmp_pos'])
    nf, ns = cmp_chunk_sums(new4[:, :, 0:2], p['cmp_pos'])
    ckv = cmp_blocks(jnp.concatenate([pf, nf], axis=1), jnp.concatenate([ps, ns], axis=1), p['cmp_proj'])
    o_c, p_cmp = nsa_compressed(q, pos, ckv)
    n_past = P // SLC_BLOCK
    n_new = s_pad // SLC_BLOCK
    idx = nsa_select(p_cmp, pos, n_past + n_new)
    sub = PAGE_SIZE // SLC_BLOCK
    pool_blk = pool.reshape((pool.shape[0], sub, SLC_BLOCK) + pool.shape[2:])
    ip = jnp.minimum(idx, n_past - 1)
    bi = jnp.arange(B)[:, None, None, None]
    gi = jnp.arange(NSA_KV)[None, None, :, None]
    page = page_table[bi, ip // sub]
    from_pool = pool_blk[page, ip % sub, :, 2:4, gi, :]
    new_blk = new4[:, :, 2:4].reshape(B, n_new, SLC_BLOCK, 2, NSA_KV, HEAD_DIM)
    from_new = gather_blocks(new_blk, jnp.clip(idx - n_past, 0, n_new - 1))
    sel = jnp.where((idx < n_past)[..., None, None, None], from_pool, from_new)
    o_s = nsa_selected(q, pos, sel, idx)
    wb = win_buf.shape[1]
    wkv = jnp.concatenate([win_buf, kv[:, :, 4:6]], axis=1)
    o_w = nsa_window(q, pos, wkv, P - wb + jnp.arange(wb + S))
    y = nsa_output(o_c, o_s, o_w, gates, p['w_out'])
    return y, kv[:, :, 0:4], tail_concat(win_buf, kv[:, :, 4:6], min(NSA_WINDOW, wb + S))


def dil_project(x, w_in, pos):
    B, T, _ = x.shape
    h = (x @ w_in).reshape(B, T, len(DIL_PAIRS), 3, DIL_HEADS, HEAD_DIM)
    q = rope(h[:, :, :, 0], pos, ROT_DIM, ROPE_THETA)
    k = rope(h[:, :, :, 1], pos, ROT_DIM, ROPE_THETA)
    return q, k, h[:, :, :, 2]


def dilated_prompt_group(q, k, v, r, span):
    B, T, H, d = q.shape
    Ls = T // r
    nb = -(-Ls // DIL_BLK)
    Lp = nb * DIL_BLK

    def sub(a):
        a = a.reshape(B, Ls, r, H, d).transpose(0, 2, 1, 3, 4)
        a = jnp.pad(a, ((0, 0), (0, 0), (0, Lp - Ls), (0, 0), (0, 0)))
        return a.reshape(B, r, nb, DIL_BLK, H, d)

    def with_prev(a):
        prev = jnp.pad(a[:, :, :-1], ((0, 0), (0, 0), (1, 0), (0, 0), (0, 0), (0, 0)))
        return jnp.concatenate([prev, a], axis=3)

    qs = sub(q)
    kw, vw = with_prev(sub(k)), with_prev(sub(v))
    s = jnp.einsum('brnqhd,brnkhd->brnhqk', qs, kw) * ATTN_SCALE
    a_i = jnp.arange(DIL_BLK)[:, None]
    b_i = jnp.arange(2 * DIL_BLK)[None, :]
    dist = a_i + DIL_BLK - b_i
    key_ok = (jnp.arange(nb)[:, None, None] > 0) | (b_i[None] >= DIL_BLK)
    mask = (dist >= 0)[None] & (dist <= span // r)[None] & key_ok
    p, lse = masked_softmax(s, mask[None, None, :, None])
    o = jnp.einsum('brnhqk,brnkhd->brnqhd', p.astype(vw.dtype), vw)
    o = o.reshape(B, r, Lp, H, d)[:, :, :Ls].transpose(0, 2, 1, 3, 4).reshape(B, T, H, d)
    lse = lse.transpose(0, 1, 2, 4, 3).reshape(B, r, Lp, H)[:, :, :Ls].transpose(0, 2, 1, 3).reshape(B, T, H)
    return o, lse


def dilated_sample_group(q, buf, k_new, v_new, r, span):
    Wb = buf.shape[1]
    S = q.shape[1]
    qi = Wb + jnp.arange(S)
    ki = qi[:, None] - jnp.arange(span // r + 1)[None, :] * r
    valid = ki >= 0
    in_buf = (ki < Wb)[None, :, :, None, None]
    ib = jnp.clip(ki, 0, Wb - 1)
    inw = jnp.clip(ki - Wb, 0, S - 1)
    kg = jnp.where(in_buf, buf[:, ib, 0], k_new[:, inw])
    vg = jnp.where(in_buf, buf[:, ib, 1], v_new[:, inw])
    s = jnp.einsum('bshd,bsjhd->bshj', q, kg) * ATTN_SCALE
    p, lse = masked_softmax(s, valid[None, :, None, :])
    o = jnp.einsum('bshj,bsjhd->bshd', p.astype(vg.dtype), vg)
    return o, lse


def dil_merge(outs, lses, w_out):
    o = jnp.stack(outs, axis=2)
    alpha = jax.nn.softmax(jnp.stack(lses, axis=2), axis=2)
    o = jnp.einsum('btgh,btghd->bthd', alpha.astype(o.dtype), o)
    B, T = o.shape[:2]
    return o.reshape(B, T, DIL_HEADS * HEAD_DIM) @ w_out


def dil_prompt(x, p):
    B, T, _ = x.shape
    q, k, v = dil_project(x, p['w_in'], jnp.arange(T))
    outs, lses, states = [], [], []
    for g, (span, r) in enumerate(DIL_PAIRS):
        o, lse = dilated_prompt_group(q[:, :, g], k[:, :, g], v[:, :, g], r, span)
        outs.append(o)
        lses.append(lse)
        states.append(jnp.stack([k[:, :, g], v[:, :, g]], axis=2)[:, T - min(span, T):])
    return dil_merge(outs, lses, p['w_out']), states


def dil_sample(x, p, bufs, past_len):
    B, S, _ = x.shape
    q, k, v = dil_project(x, p['w_in'], past_len + jnp.arange(S))
    outs, lses, states = [], [], []
    for g, ((span, r), buf) in enumerate(zip(DIL_PAIRS, bufs)):
        o, lse = dilated_sample_group(q[:, :, g], buf, k[:, :, g], v[:, :, g], r, span)
        outs.append(o)
        lses.append(lse)
        states.append(tail_concat(buf, jnp.stack([k[:, :, g], v[:, :, g]], axis=2), min(span, buf.shape[1] + S)))
    return dil_merge(outs, lses, p['w_out']), states


def mla_project(x, p, pos):
    B, T, _ = x.shape
    h = x @ p['w_in']
    c_q = rms_norm(h[..., :MLA_Q_RANK], p['q_norm'])
    c_kv = rms_norm(h[..., MLA_Q_RANK:MLA_Q_RANK + MLA_KV_RANK], p['kv_norm'])
    k_pe = rope(h[..., MLA_Q_RANK + MLA_KV_RANK:], pos, MLA_ROPE, MLA_THETA)
    q = (c_q @ p['w_uq']).reshape(B, T, MLA_HEADS, MLA_NOPE + MLA_ROPE)
    q_pe = rope(q[..., MLA_NOPE:], pos, MLA_ROPE, MLA_THETA)
    return q[..., :MLA_NOPE], q_pe, c_kv, k_pe


def mla_prompt(x, p):
    B, T, _ = x.shape
    q_nope, q_pe, c_kv, k_pe = mla_project(x, p, jnp.arange(T))
    kv = (c_kv @ p['w_ukv']).reshape(B, T, MLA_HEADS, MLA_NOPE + MLA_V)
    k_nope, v = kv[..., :MLA_NOPE], kv[..., MLA_NOPE:]
    nb = T // QBLK
    qn = q_nope.reshape(B, nb, QBLK, MLA_HEADS, MLA_NOPE).swapaxes(0, 1)
    qp = q_pe.reshape(B, nb, QBLK, MLA_HEADS, MLA_ROPE).swapaxes(0, 1)
    kpos = jnp.arange(T)

    def block(args):
        n, qn_b, qp_b = args
        s = (jnp.einsum('bqhc,bkhc->bhqk', qn_b, k_nope)
             + jnp.einsum('bqhr,bkr->bhqk', qp_b, k_pe)) * MLA_SCALE
        mask = kpos[None, :] <= (n * QBLK + jnp.arange(QBLK))[:, None]
        pr, _ = masked_softmax(s, mask)
        return jnp.einsum('bhqk,bkhd->bqhd', pr.astype(v.dtype), v)

    o = lax.map(block, (jnp.arange(nb), qn, qp)).swapaxes(0, 1).reshape(B, T, MLA_HEADS * MLA_V)
    return o @ p['w_out'], jnp.concatenate([c_kv, k_pe], axis=-1)


def mla_sample(x, p, pool, page_table):
    B, S, _ = x.shape
    P = page_table.shape[1] * PAGE_SIZE
    pos = P + jnp.arange(S)
    q_nope, q_pe, c_kv, k_pe = mla_project(x, p, pos)
    w_ukv = p['w_ukv'].reshape(MLA_KV_RANK, MLA_HEADS, MLA_NOPE + MLA_V)
    q_lat = jnp.einsum('bshc,lhc->bshl', q_nope, w_ukv[..., :MLA_NOPE])
    past_c = pool[page_table, :, :MLA_KV_RANK].reshape(B, P, MLA_KV_RANK)
    past_pe = pool[page_table, :, MLA_KV_RANK:].reshape(B, P, MLA_ROPE)
    s_past = jnp.einsum('bshl,btl->bsht', q_lat, past_c) + jnp.einsum('bshr,btr->bsht', q_pe, past_pe)
    s_new = jnp.einsum('bshl,btl->bsht', q_lat, c_kv) + jnp.einsum('bshr,btr->bsht', q_pe, k_pe)
    s = jnp.concatenate([s_past, s_new], axis=-1) * MLA_SCALE
    mask = (jnp.arange(P + S)[None, :] <= pos[:, None])[None, :, None, :]
    pr, _ = masked_softmax(s, mask)
    pr = pr.astype(c_kv.dtype)
    o_lat = (jnp.einsum('bsht,btl->bshl', pr[..., :P], past_c)
             + jnp.einsum('bsht,btl->bshl', pr[..., P:], c_kv))
    o = jnp.einsum('bshl,lhd->bshd', o_lat, w_ukv[..., MLA_NOPE:]).reshape(B, S, MLA_HEADS * MLA_V)
    return o @ p['w_out'], jnp.concatenate([c_kv, k_pe], axis=-1)


def setup_inputs(seed: int = 0) -> dict:
    key = jax.random.key(seed)
    keys = iter(jax.random.split(key, 128))

    def normal(shape, scale):
        return jax.random.normal(next(keys), shape, jnp.float32) * scale

    def gain(n):
        return 1.0 + normal((n,), 0.1)

    n_pages = PAST_LEN // PAGE_SIZE
    n_used = DEC_BATCH * n_pages
    n_phys = n_used + n_used // 4
    inp = {}
    inp['x_prompt'] = normal((BATCH, SEQ, D_MODEL), 1.0)
    inp['x_sample'] = normal((DEC_BATCH, DEC_SEQ, D_MODEL), 1.0)
    for i in range(DEPTH):
        kind = i % N_MIXERS
        if kind == 0:
            inp[f'cache_l{i}_nsa'] = normal((n_phys, PAGE_SIZE, 4, NSA_KV, HEAD_DIM), 1.0)
            inp[f'cache_l{i}_win'] = normal((DEC_BATCH, min(NSA_WINDOW, PAST_LEN), 2, NSA_KV, HEAD_DIM), 1.0)
        elif kind == 1:
            for span, _ in DIL_PAIRS:
                inp[f'cache_l{i}_dil_w{span}'] = normal((DEC_BATCH, min(span, PAST_LEN), 2, DIL_HEADS, HEAD_DIM), 1.0)
        else:
            inp[f'cache_l{i}_mla'] = normal((n_phys, PAGE_SIZE, MLA_KV_RANK + MLA_ROPE), 1.0)
    perm = jax.random.permutation(next(keys), n_phys)[:n_used]
    inp['page_table'] = perm.reshape(DEC_BATCH, n_pages).astype(jnp.int32)
    for i in range(DEPTH):
        kind = i % N_MIXERS
        inp[f'l{i}_norm_attn'] = gain(D_MODEL)
        if kind == 0:
            inp[f'l{i}_w_in'] = normal((D_MODEL, NSA_IN), D_MODEL ** -0.5)
            inp[f'l{i}_cmp_pos'] = (1.0 + normal((2, CMP_BLOCK, HEAD_DIM), 0.1)) * CMP_BLOCK ** -0.5
            inp[f'l{i}_cmp_proj'] = normal((2, HEAD_DIM, HEAD_DIM), HEAD_DIM ** -0.5)
            inp[f'l{i}_w_out'] = normal((NSA_HEADS * HEAD_DIM, D_MODEL), (NSA_HEADS * HEAD_DIM) ** -0.5)
        elif kind == 1:
            inp[f'l{i}_w_in'] = normal((D_MODEL, DIL_IN), D_MODEL ** -0.5)
            inp[f'l{i}_w_out'] = normal((DIL_HEADS * HEAD_DIM, D_MODEL), (DIL_HEADS * HEAD_DIM) ** -0.5)
        else:
            inp[f'l{i}_w_in'] = normal((D_MODEL, MLA_IN), D_MODEL ** -0.5)
            inp[f'l{i}_q_norm'] = gain(MLA_Q_RANK)
            inp[f'l{i}_w_uq'] = normal((MLA_Q_RANK, MLA_HEADS * (MLA_NOPE + MLA_ROPE)), MLA_Q_RANK ** -0.5)
            inp[f'l{i}_kv_norm'] = gain(MLA_KV_RANK)
            inp[f'l{i}_w_ukv'] = normal((MLA_KV_RANK, MLA_HEADS * (MLA_NOPE + MLA_V)), MLA_KV_RANK ** -0.5)
            inp[f'l{i}_w_out'] = normal((MLA_HEADS * MLA_V, D_MODEL), (MLA_HEADS * MLA_V) ** -0.5)
        inp[f'l{i}_norm_mlp'] = gain(D_MODEL)
        inp[f'l{i}_w_up'] = normal((D_MODEL, D_FF), D_MODEL ** -0.5)
        inp[f'l{i}_w_down'] = normal((D_FF, D_MODEL), D_FF ** -0.5)
    inp['final_norm'] = gain(D_MODEL)
    return inp


def reference(x_prompt, x_sample, cache_l0_nsa, cache_l0_win, cache_l1_dil_w128, cache_l1_dil_w512,
              cache_l1_dil_w2048, cache_l2_mla, cache_l3_nsa, cache_l3_win, page_table,
              l0_norm_attn, l0_w_in, l0_cmp_pos, l0_cmp_proj, l0_w_out, l0_norm_mlp, l0_w_up, l0_w_down,
              l1_norm_attn, l1_w_in, l1_w_out, l1_norm_mlp, l1_w_up, l1_w_down,
              l2_norm_attn, l2_w_in, l2_q_norm, l2_w_uq, l2_kv_norm, l2_w_ukv, l2_w_out, l2_norm_mlp, l2_w_up, l2_w_down,
              l3_norm_attn, l3_w_in, l3_cmp_pos, l3_cmp_proj, l3_w_out, l3_norm_mlp, l3_w_up, l3_w_down,
              final_norm):
    layers = [
        dict(norm_attn=l0_norm_attn, w_in=l0_w_in, cmp_pos=l0_cmp_pos, cmp_proj=l0_cmp_proj, w_out=l0_w_out,
             norm_mlp=l0_norm_mlp, w_up=l0_w_up, w_down=l0_w_down),
        dict(norm_attn=l1_norm_attn, w_in=l1_w_in, w_out=l1_w_out,
             norm_mlp=l1_norm_mlp, w_up=l1_w_up, w_down=l1_w_down),
        dict(norm_attn=l2_norm_attn, w_in=l2_w_in, q_norm=l2_q_norm, w_uq=l2_w_uq, kv_norm=l2_kv_norm,
             w_ukv=l2_w_ukv, w_out=l2_w_out, norm_mlp=l2_norm_mlp, w_up=l2_w_up, w_down=l2_w_down),
        dict(norm_attn=l3_norm_attn, w_in=l3_w_in, cmp_pos=l3_cmp_pos, cmp_proj=l3_cmp_proj, w_out=l3_w_out,
             norm_mlp=l3_norm_mlp, w_up=l3_w_up, w_down=l3_w_down),
    ]
    caches = [
        dict(nsa=cache_l0_nsa, win=cache_l0_win),
        dict(dil=(cache_l1_dil_w128, cache_l1_dil_w512, cache_l1_dil_w2048)),
        dict(mla=cache_l2_mla),
        dict(nsa=cache_l3_nsa, win=cache_l3_win),
    ]
    past_len = page_table.shape[1] * PAGE_SIZE
    h_p, h_s = x_prompt, x_sample
    new_state = []
    for i in range(DEPTH):
        p = layers[i]
        c = caches[i]
        kind = i % N_MIXERS
        a_p = rms_norm(h_p, p['norm_attn'])
        a_s = rms_norm(h_s, p['norm_attn'])
        if kind == 0:
            y_p, nsa_p, win_p = nsa_prompt(a_p, p)
            y_s, nsa_s, win_s = nsa_sample(a_s, p, c['nsa'], c['win'], page_table)
            new_state += [nsa_p, nsa_s, win_p, win_s]
        elif kind == 1:
            y_p, st_p = dil_prompt(a_p, p)
            y_s, st_s = dil_sample(a_s, p, c['dil'], past_len)
            for sp, ss in zip(st_p, st_s):
                new_state += [sp, ss]
        else:
            y_p, mla_p = mla_prompt(a_p, p)
            y_s, mla_s = mla_sample(a_s, p, c['mla'], page_table)
            new_state += [mla_p, mla_s]
        h_p = h_p + y_p
        h_s = h_s + y_s
        h_p = h_p + sqrelu_mlp(rms_norm(h_p, p['norm_mlp']), p['w_up'], p['w_down'])
        h_s = h_s + sqrelu_mlp(rms_norm(h_s, p['norm_mlp']), p['w_up'], p['w_down'])
    y_prompt = rms_norm(h_p, final_norm)
    y_sample = rms_norm(h_s, final_norm)
    (l0_nsa_p, l0_nsa_s, l0_win_p, l0_win_s,
     l1_w128_p, l1_w128_s, l1_w512_p, l1_w512_s, l1_w2048_p, l1_w2048_s,
     l2_mla_p, l2_mla_s,
     l3_nsa_p, l3_nsa_s, l3_win_p, l3_win_s) = new_state
    return (y_prompt, y_sample, l0_nsa_p, l0_nsa_s, l0_win_p, l0_win_s,
            l1_w128_p, l1_w128_s, l1_w512_p, l1_w512_s, l1_w2048_p, l1_w2048_s,
            l2_mla_p, l2_mla_s, l3_nsa_p, l3_nsa_s, l3_win_p, l3_win_s)
```

```python
import functools

import jax
import jax.numpy as jnp
import numpy as np
from jax import lax
from jax.experimental import pallas as pl
from jax.experimental.pallas import tpu as pltpu

F32 = jnp.float32
BF16 = jnp.bfloat16
NEG_INF = float("-inf")

HEAD_DIM = 64
ROT_DIM = HEAD_DIM // 4
ROPE_THETA = 500000.0
NORM_EPS = 1e-6
NSA_HEADS = 16
NSA_KV = 2
NSA_HPG = NSA_HEADS // NSA_KV
CMP_BLOCK = 32
CMP_STRIDE = 16
SLC_BLOCK = 64
SLC_TOPK = 16
NSA_WINDOW = 512
DIL_PAIRS = ((128, 1), (512, 4), (2048, 16))
DIL_HEADS = 8
MLA_HEADS = 16
MLA_NOPE = 64
MLA_ROPE = 32
MLA_V = 64
MLA_KV_RANK = 256
MLA_Q_RANK = 384
MLA_THETA = 10000.0
MLA_SCALE = (MLA_NOPE + MLA_ROPE) ** -0.5

LANES = 128
VMEM_LIMIT = 56 * 1024 * 1024


def _cparams(sem):
    return pltpu.CompilerParams(dimension_semantics=sem, vmem_limit_bytes=VMEM_LIMIT)


def _rms(x, gain):
    y = x * lax.rsqrt(jnp.mean(x * x, axis=-1, keepdims=True) + NORM_EPS)
    return y * gain


def _rope(x, cos, sa, sb, half):
    return x * cos + pltpu.roll(x, LANES - half, 1) * sa + pltpu.roll(x, half, 1) * sb


def _div_p2(x, n):
    assert n > 0 and n & (n - 1) == 0
    return x >> (n.bit_length() - 1)


def _mod_p2(x, n):
    assert n > 0 and n & (n - 1) == 0
    return x & (n - 1)


def _split3(x):
    hi = x.astype(BF16)
    r1 = x - hi.astype(F32)
    mid = r1.astype(BF16)
    lo = (r1 - mid.astype(F32)).astype(BF16)
    return hi, mid, lo


def _dot3(x, e):
    hi, mid, lo = _split3(x)
    d = lambda a: jnp.dot(a, e, preferred_element_type=F32)
    return d(hi) + d(mid) + d(lo)


def _dot_nt(a, b):
    return lax.dot_general(a, b, (((1,), (1,)), ((), ())), preferred_element_type=F32)


def _masked_softmax(s, mask):
    s = jnp.where(mask, s, NEG_INF)
    m = jnp.max(s, axis=-1, keepdims=True)
    m = jnp.where(m > NEG_INF, m, 0.0)
    e = jnp.exp(s - m)
    l = jnp.sum(e, axis=-1, keepdims=True)
    p = e / jnp.maximum(l, 1e-30)
    return p, m, l


def _online_update(s, v, m_ref, l_ref, acc_ref):
    m_old = m_ref[...]
    m_new = jnp.maximum(m_old, jnp.max(s, axis=-1, keepdims=True))
    m_safe = jnp.where(m_new > NEG_INF, m_new, 0.0)
    alpha = jnp.exp(m_old - m_safe)
    p = jnp.exp(s - m_safe)
    l_ref[...] = alpha * l_ref[...] + jnp.sum(p, axis=-1, keepdims=True)
    acc_ref[...] = alpha * acc_ref[...] + jnp.dot(p.astype(BF16), v, preferred_element_type=F32)
    m_ref[...] = m_new


def _online_init(m_ref, l_ref, acc_ref):
    m_ref[...] = jnp.full(m_ref.shape, NEG_INF, F32)
    l_ref[...] = jnp.zeros(l_ref.shape, F32)
    acc_ref[...] = jnp.zeros(acc_ref.shape, F32)


def _rope_tables(pos, rot, theta, period, offset):
    half = rot // 2
    inv = theta ** (-(jnp.arange(half, dtype=F32) * 2.0) / rot)
    ang = pos.astype(F32)[:, None] * inv[None, :]
    cos, sin = jnp.cos(ang), jnp.sin(ang)
    lane = np.arange(LANES)
    i = lane % period - offset
    first = (i >= 0) & (i < half)
    second = (i >= half) & (i < rot)
    idx = np.where(first, i, np.where(second, i - half, 0))
    cos_t = jnp.where(jnp.asarray(first | second)[None, :], cos[:, idx], 1.0)
    sa_t = jnp.where(jnp.asarray(first)[None, :], -sin[:, idx], 0.0)
    sb_t = jnp.where(jnp.asarray(second)[None, :], sin[:, idx], 0.0)
    return cos_t, sa_t, sb_t


def _row_tile(m, pref):
    t = min(m, pref)
    while m % t:
        t //= 2
    return t


NSA_QW = NSA_HEADS * HEAD_DIM
NSA_KVW = 6 * NSA_KV * HEAD_DIM
NSA_NP = NSA_QW + NSA_KVW + LANES


def _nsa_weights(w_in, w_out):
    nq, nkv = NSA_QW, NSA_KVW
    perm = []
    for j in range(NSA_HPG):
        perm += list(range(j * HEAD_DIM, (j + 1) * HEAD_DIM))
        perm += list(range((NSA_HPG + j) * HEAD_DIM, (NSA_HPG + j + 1) * HEAD_DIM))
    perm = np.asarray(perm)
    ng = 3 * NSA_HEADS
    w = jnp.concatenate([w_in[:, :nq][:, perm], w_in[:, nq:nq + nkv], w_in[:, nq + nkv:],
                         jnp.zeros((w_in.shape[0], LANES - ng), w_in.dtype)], axis=1)
    return w.astype(BF16), w_out[perm, :].astype(BF16)


def _nsa_proj_kernel(x_ref, g_ref, w_ref, cos_ref, sa_ref, sb_ref, q_ref, kvf_ref, kvb_ref, gate_ref):
    xn = _rms(x_ref[...], g_ref[...]).astype(BF16)
    cos, sa, sb = cos_ref[...], sa_ref[...], sb_ref[...]
    half = ROT_DIM // 2
    nqb = NSA_QW // LANES
    nkb = NSA_KVW // LANES
    for c in range(0, nqb + nkb + 1, 2):
        nb = min(2, nqb + nkb + 1 - c)
        h = jnp.dot(xn, w_ref[:, c * LANES:(c + nb) * LANES], preferred_element_type=F32)
        for u in range(nb):
            blk = c + u
            hb = h[:, u * LANES:(u + 1) * LANES]
            if blk < nqb:
                q = _rope(hb, cos, sa, sb, half) * (HEAD_DIM ** -0.5)
                q_ref[:, blk * LANES:(blk + 1) * LANES] = q.astype(BF16)
            elif blk < nqb + nkb:
                kb = blk - nqb
                if kb % 2 == 0:
                    hb = _rope(hb, cos, sa, sb, half)
                kvf_ref[:, kb * LANES:(kb + 1) * LANES] = hb
                kvb_ref[:, kb * LANES:(kb + 1) * LANES] = hb.astype(BF16)
            else:
                gate_ref[...] = 1.0 / (1.0 + jnp.exp(-hb))


def _nsa_proj(x, gain, w, tabs, tab_blocks, tm):
    m, d = x.shape
    row = lambda i: (i, 0)
    tab = lambda i: (i % tab_blocks, 0)
    const = lambda i: (0, 0)
    return pl.pallas_call(
        _nsa_proj_kernel,
        grid=(m // tm,),
        in_specs=[pl.BlockSpec((tm, d), row), pl.BlockSpec((1, d), const), pl.BlockSpec(w.shape, const),
                  pl.BlockSpec((tm, LANES), tab), pl.BlockSpec((tm, LANES), tab), pl.BlockSpec((tm, LANES), tab)],
        out_specs=[pl.BlockSpec((tm, NSA_QW), row), pl.BlockSpec((tm, NSA_KVW), row),
                   pl.BlockSpec((tm, NSA_KVW), row), pl.BlockSpec((tm, LANES), row)],
        out_shape=[jax.ShapeDtypeStruct((m, NSA_QW), BF16), jax.ShapeDtypeStruct((m, NSA_KVW), F32),
                   jax.ShapeDtypeStruct((m, NSA_KVW), BF16), jax.ShapeDtypeStruct((m, LANES), F32)],
        compiler_params=_cparams(("parallel",)),
        name="nsa_proj",
    )(x, gain.reshape(1, d), w, *tabs)


def _cmp_weights(cmp_pos, cmp_proj):
    w = jnp.concatenate([cmp_pos[0], cmp_pos[0], cmp_pos[1], cmp_pos[1]], axis=-1)
    z = jnp.zeros((HEAD_DIM, HEAD_DIM), cmp_proj.dtype)
    rows = []
    for t in range(4):
        p = cmp_proj[t // 2]
        rows.append(jnp.concatenate([p if u == t else z for u in range(4)], axis=1))
    return w.astype(F32), jnp.concatenate(rows, axis=0).astype(BF16)


def _chunk_sums(x, w):
    n = x.shape[0] // CMP_STRIDE
    x3 = x.reshape(n, CMP_STRIDE, x.shape[1])
    first = jnp.sum(x3 * w[:CMP_STRIDE][None], axis=1)
    second = jnp.sum(x3 * w[CMP_STRIDE:][None], axis=1)
    return first, second


def _cmp_prompt_kernel(kv_ref, w_ref, proj_ref, o_ref, f_ref, s_ref):
    t = kv_ref.shape[0]
    n = t // CMP_STRIDE
    w = w_ref[...]
    rows = 256
    for c in range(t // rows):
        first, second = _chunk_sums(kv_ref[c * rows:(c + 1) * rows, 0:2 * LANES], w)
        nn = rows // CMP_STRIDE
        f_ref[c * nn:(c + 1) * nn] = first
        s_ref[c * nn:(c + 1) * nn] = second
    nxt = pltpu.roll(s_ref[...], n - 1, 0)
    ridx = lax.broadcasted_iota(jnp.int32, (n, 2 * LANES), 0)
    s = jnp.where(ridx < n - 1, f_ref[...] + nxt, 0.0)
    o_ref[...] = jnp.dot(s.astype(BF16), proj_ref[...], preferred_element_type=F32).astype(BF16)


def _cmp_prompt(kvf, w, proj, b, t):
    n = t // CMP_STRIDE
    kv3 = kvf.reshape(b, t, NSA_KVW)
    return pl.pallas_call(
        _cmp_prompt_kernel,
        grid=(b,),
        in_specs=[pl.BlockSpec((None, t, NSA_KVW), lambda i: (i, 0, 0)),
                  pl.BlockSpec(w.shape, lambda i: (0, 0)), pl.BlockSpec(proj.shape, lambda i: (0, 0))],
        out_specs=pl.BlockSpec((None, n, 2 * LANES), lambda i: (i, 0, 0)),
        out_shape=jax.ShapeDtypeStruct((b, n, 2 * LANES), BF16),
        scratch_shapes=[pltpu.VMEM((n, 2 * LANES), F32), pltpu.VMEM((n, 2 * LANES), F32)],
        compiler_params=_cparams(("parallel",)),
        name="nsa_cmp_blocks",
    )(kv3, w, proj)


def _select_matrix(n_cmp_pad, n_sel):
    i = np.arange(n_cmp_pad)[:, None]
    j = np.arange(LANES if n_sel <= LANES else 2 * LANES)[None, :]
    s = ((i // 4 == j) | (i == 4 * j - 1)) & (j < n_sel)
    return jnp.asarray(s, BF16)


def _topk_mask(sc, n_sel, k):
    lane = lax.broadcasted_iota(jnp.int32, sc.shape, 1)
    rank = jnp.zeros(sc.shape, F32)
    for i in range(n_sel):
        col = sc[:, i:i + 1]
        beats = (col > sc) | ((col == sc) & (lane > i))
        rank = rank + jnp.where(beats, 1.0, 0.0)
    return jnp.where((rank < k) & (lane < n_sel), 1.0, 0.0)


def _nsa_attn_kernel(q_ref, kv_ref, ckv_ref, gate_ref, smat_ref, o_ref,
                     qs_ref, sel_ref, m_ref, l_ref, acc_ref, oc_ref, os_ref, ow_ref, *, tq, tk, n_sel):
    i = pl.program_id(1)
    q0 = i * tq
    hq = NSA_HPG * tq
    lane1 = lax.broadcasted_iota(jnp.int32, (1, LANES), 1)
    low = lane1 < HEAD_DIM

    for j in range(NSA_HPG):
        q2 = q_ref[:, j * LANES:(j + 1) * LANES]
        zero = jnp.zeros_like(q2)
        qs_ref[j * tq:(j + 1) * tq] = jnp.where(low, q2, zero)
        qs_ref[(NSA_HPG + j) * tq:(NSA_HPG + j + 1) * tq] = jnp.where(low, zero, q2)

    nc = ckv_ref.shape[0]
    ck2 = ckv_ref[:, 0:LANES]
    cv2 = ckv_ref[:, LANES:2 * LANES]
    qp3 = q0 + lax.broadcasted_iota(jnp.int32, (1, tq, nc), 1)
    n3 = lax.broadcasted_iota(jnp.int32, (1, tq, nc), 2)
    cmask = (n3 * CMP_STRIDE + CMP_BLOCK - 1) <= qp3
    qp2 = q0 + lax.broadcasted_iota(jnp.int32, (tq, LANES), 0)
    blk2 = lax.broadcasted_iota(jnp.int32, (tq, LANES), 1)
    cur = qp2 >> 6
    forced = (blk2 == 0) | (blk2 == cur) | (blk2 == cur - 1)
    future = (blk2 * SLC_BLOCK) > qp2
    for g in range(NSA_KV):
        qg = qs_ref[g * hq:(g + 1) * hq]
        s = _dot_nt(qg, ck2).reshape(NSA_HPG, tq, nc)
        p, _, _ = _masked_softmax(s, cmask)
        oc_ref[g * hq:(g + 1) * hq] = jnp.dot(p.astype(BF16).reshape(hq, nc), cv2, preferred_element_type=F32)
        imp = jnp.sum(p, axis=0)
        sc = _dot3(imp, smat_ref[...])
        sc = jnp.where(forced, jnp.inf, jnp.where(future, NEG_INF, sc))
        sc = jnp.where(blk2 < n_sel, sc, NEG_INF)
        sel_ref[g] = _topk_mask(sc, n_sel, min(SLC_TOPK, n_sel)).astype(BF16)

    n_kt = (q0 + tq + tk - 1) // tk
    qp_k = q0 + lax.broadcasted_iota(jnp.int32, (tq, tk), 0)
    col_k = lax.broadcasted_iota(jnp.int32, (tq, tk), 1)
    eb_r = lax.broadcasted_iota(jnp.int32, (LANES, tk), 0)
    eb_c = lax.broadcasted_iota(jnp.int32, (LANES, tk), 1)
    for g in range(NSA_KV):
        rows = slice(g * hq, (g + 1) * hq)
        _online_init(m_ref.at[rows], l_ref.at[rows], acc_ref.at[rows])

        def sel_body(t, carry, g=g, rows=rows):
            k0 = pl.multiple_of(t * tk, tk)
            k2 = kv_ref[pl.ds(k0, tk), 2 * LANES:3 * LANES]
            v2 = kv_ref[pl.ds(k0, tk), 3 * LANES:4 * LANES]
            expand = jnp.where(eb_r == ((k0 + eb_c) >> 6), 1.0, 0.0).astype(BF16)
            selx = jnp.dot(sel_ref[g], expand, preferred_element_type=F32)
            mask = (selx > 0.5) & ((k0 + col_k) <= qp_k)
            s = _dot_nt(qs_ref[rows], k2).reshape(NSA_HPG, tq, tk)
            s = jnp.where(mask[None], s, NEG_INF).reshape(hq, tk)
            _online_update(s, v2, m_ref.at[rows], l_ref.at[rows], acc_ref.at[rows])
            return carry

        lax.fori_loop(0, n_kt, sel_body, 0)
        os_ref[rows] = acc_ref[rows] / jnp.maximum(l_ref[rows], 1e-30)

    _online_init(m_ref, l_ref, acc_ref)
    t_lo = jnp.maximum(q0 - NSA_WINDOW, 0) // tk

    def win_body(t, carry):
        k0 = pl.multiple_of(t * tk, tk)
        k2 = kv_ref[pl.ds(k0, tk), 4 * LANES:5 * LANES]
        v2 = kv_ref[pl.ds(k0, tk), 5 * LANES:6 * LANES]
        dist = qp_k - (k0 + col_k)
        mask = (dist >= 0) & (dist <= NSA_WINDOW)
        s = _dot_nt(qs_ref[...], k2).reshape(NSA_HEADS, tq, tk)
        s = jnp.where(mask[None], s, NEG_INF).reshape(NSA_HEADS * tq, tk)
        _online_update(s, v2, m_ref, l_ref, acc_ref)
        return carry

    lax.fori_loop(t_lo, n_kt, win_body, 0)
    ow_ref[...] = acc_ref[...] / jnp.maximum(l_ref[...], 1e-30)

    for j in range(NSA_HPG):
        parts = []
        for g in range(NSA_KV):
            h = g * NSA_HPG + j
            r = slice(h * tq, (h + 1) * tq)
            parts.append(gate_ref[:, h:h + 1] * oc_ref[r]
                         + gate_ref[:, NSA_HEADS + h:NSA_HEADS + h + 1] * os_ref[r]
                         + gate_ref[:, 2 * NSA_HEADS + h:2 * NSA_HEADS + h + 1] * ow_ref[r])
        o_ref[:, j * LANES:(j + 1) * LANES] = jnp.where(low, parts[0], parts[1]).astype(BF16)


def _nsa_attn(q, kvb, ckv, gates, b, t, tq=128, tk=128):
    n_sel = t // SLC_BLOCK
    nc = ckv.shape[1]
    smat = _select_matrix(nc, n_sel)
    rows = NSA_HEADS * tq
    kern = functools.partial(_nsa_attn_kernel, tq=tq, tk=tk, n_sel=n_sel)
    return pl.pallas_call(
        kern,
        grid=(b, t // tq),
        in_specs=[pl.BlockSpec((None, tq, NSA_QW), lambda bi, i: (bi, i, 0)),
                  pl.BlockSpec((None, t, NSA_KVW), lambda bi, i: (bi, 0, 0)),
                  pl.BlockSpec((None, nc, 2 * LANES), lambda bi, i: (bi, 0, 0)),
                  pl.BlockSpec((None, tq, LANES), lambda bi, i: (bi, i, 0)),
                  pl.BlockSpec(smat.shape, lambda bi, i: (0, 0))],
        out_specs=pl.BlockSpec((None, tq, NSA_QW), lambda bi, i: (bi, i, 0)),
        out_shape=jax.ShapeDtypeStruct((b, t, NSA_QW), BF16),
        scratch_shapes=[pltpu.VMEM((rows, LANES), BF16), pltpu.VMEM((NSA_KV, tq, LANES), BF16),
                        pltpu.VMEM((rows, 1), F32), pltpu.VMEM((rows, 1), F32), pltpu.VMEM((rows, LANES), F32),
                        pltpu.VMEM((rows, LANES), F32), pltpu.VMEM((rows, LANES), F32),
                        pltpu.VMEM((rows, LANES), F32)],
        compiler_params=_cparams(("parallel", "parallel")),
        name="nsa_attn",
    )(q.reshape(b, t, NSA_QW), kvb.reshape(b, t, NSA_KVW), ckv, gates.reshape(b, t, LANES), smat)


def _out_proj_kernel(o_ref, w_ref, r_ref, y_ref):
    y_ref[...] = r_ref[...] + jnp.dot(o_ref[...], w_ref[...], preferred_element_type=F32)


def _out_proj(o, w, res, tm):
    m, k = o.shape
    d = w.shape[1]
    return pl.pallas_call(
        _out_proj_kernel,
        grid=(m // tm,),
        in_specs=[pl.BlockSpec((tm, k), lambda i: (i, 0)), pl.BlockSpec(w.shape, lambda i: (0, 0)),
                  pl.BlockSpec((tm, d), lambda i: (i, 0))],
        out_specs=pl.BlockSpec((tm, d), lambda i: (i, 0)),
        out_shape=jax.ShapeDtypeStruct((m, d), F32),
        compiler_params=_cparams(("parallel",)),
        name="out_proj",
    )(o, w, res)


def _mlp_kernel(x_ref, g_ref, wu_ref, wd_ref, fg_ref, y_ref, xn_ref, acc_ref, *, final_norm):
    j = pl.program_id(1)

    @pl.when(j == 0)
    def _():
        xn_ref[...] = _rms(x_ref[...], g_ref[...]).astype(BF16)
        acc_ref[...] = x_ref[...]

    h = jnp.maximum(jnp.dot(xn_ref[...], wu_ref[...], preferred_element_type=F32), 0.0)
    acc_ref[...] += jnp.dot((h * h).astype(BF16), wd_ref[...], preferred_element_type=F32)

    @pl.when(j == pl.num_programs(1) - 1)
    def _():
        y = acc_ref[...]
        if final_norm:
            y = _rms(y, fg_ref[...])
        y_ref[...] = y


def _mlp(x, gain, w_up, w_down, final_gain, tm, tf=512):
    m, d = x.shape
    ff = w_up.shape[1]
    final_norm = final_gain is not None
    fg = (final_gain if final_norm else gain).reshape(1, d)
    kern = functools.partial(_mlp_kernel, final_norm=final_norm)
    return pl.pallas_call(
        kern,
        grid=(m // tm, ff // tf),
        in_specs=[pl.BlockSpec((tm, d), lambda i, j: (i, 0)), pl.BlockSpec((1, d), lambda i, j: (0, 0)),
                  pl.BlockSpec((d, tf), lambda i, j: (0, j)), pl.BlockSpec((tf, d), lambda i, j: (j, 0)),
                  pl.BlockSpec((1, d), lambda i, j: (0, 0))],
        out_specs=pl.BlockSpec((tm, d), lambda i, j: (i, 0)),
        out_shape=jax.ShapeDtypeStruct((m, d), F32),
        scratch_shapes=[pltpu.VMEM((tm, d), BF16), pltpu.VMEM((tm, d), F32)],
        compiler_params=_cparams(("parallel", "arbitrary")),
        name="mlp",
    )(x, gain.reshape(1, d), w_up, w_down, fg)


DIL_GW = DIL_HEADS * HEAD_DIM
DIL_NG = len(DIL_PAIRS)


def _dil_proj_kernel(x_ref, g_ref, w_ref, cos_ref, sa_ref, sb_ref, hb_ref, st_ref, xn_ref):
    j = pl.program_id(1)

    @pl.when(j == 0)
    def _():
        xn_ref[...] = _rms(x_ref[...], g_ref[...]).astype(BF16)

    c = j % 3
    rot = c < 2
    cos = jnp.where(rot, cos_ref[...], 1.0)
    sa = jnp.where(rot, sa_ref[...], 0.0)
    sb = jnp.where(rot, sb_ref[...], 0.0)
    scale = jnp.where(c == 0, HEAD_DIM ** -0.5, 1.0)
    h = jnp.dot(xn_ref[...], w_ref[...], preferred_element_type=F32)
    for u in range(DIL_GW // LANES):
        r = _rope(h[:, u * LANES:(u + 1) * LANES], cos, sa, sb, ROT_DIM // 2)
        hb_ref[:, u * LANES:(u + 1) * LANES] = (r * scale).astype(BF16)

        @pl.when(c > 0)
        def _():
            st_ref[:, u * LANES:(u + 1) * LANES] = r


def _dil_proj(x, gain, w, tabs, tab_blocks, tm):
    m, d = x.shape
    n = w.shape[1]
    nj = n // DIL_GW
    tab = lambda i, j: (i % tab_blocks, 0)
    st_map = lambda i, j: (i, (j // 3) * 2 + jnp.maximum(j % 3 - 1, 0))
    return pl.pallas_call(
        _dil_proj_kernel,
        grid=(m // tm, nj),
        in_specs=[pl.BlockSpec((tm, d), lambda i, j: (i, 0)), pl.BlockSpec((1, d), lambda i, j: (0, 0)),
                  pl.BlockSpec((d, DIL_GW), lambda i, j: (0, j)),
                  pl.BlockSpec((tm, LANES), tab), pl.BlockSpec((tm, LANES), tab), pl.BlockSpec((tm, LANES), tab)],
        out_specs=[pl.BlockSpec((tm, DIL_GW), lambda i, j: (i, j)), pl.BlockSpec((tm, DIL_GW), st_map)],
        out_shape=[jax.ShapeDtypeStruct((m, n), BF16), jax.ShapeDtypeStruct((m, 2 * DIL_NG * DIL_GW), F32)],
        scratch_shapes=[pltpu.VMEM((tm, d), BF16)],
        compiler_params=_cparams(("parallel", "arbitrary")),
        name="dil_proj",
    )(x, gain.reshape(1, d), w, *tabs)


def _dil_attn_kernel(q_ref, k_ref, v_ref, o_ref, lse_ref, *, tq, win, ls):
    i = pl.program_id(2)
    nk = min(2 * tq, ls)
    start = pl.multiple_of(jnp.clip((i - 1) * tq, 0, ls - nk), tq)
    low = lax.broadcasted_iota(jnp.int32, (1, LANES), 1) < HEAD_DIM
    qp = i * tq + _mod_p2(lax.broadcasted_iota(jnp.int32, (2 * tq, nk), 0), tq)
    kp = start + lax.broadcasted_iota(jnp.int32, (2 * tq, nk), 1)
    dist = qp - kp
    mask = (dist >= 0) & (dist <= win)
    for jp in range(DIL_GW // LANES):
        cols = slice(jp * LANES, (jp + 1) * LANES)
        q2 = q_ref[:, cols]
        zero = jnp.zeros_like(q2)
        qs = jnp.concatenate([jnp.where(low, q2, zero), jnp.where(low, zero, q2)], axis=0)
        k2 = k_ref[pl.ds(start, nk), cols]
        v2 = v_ref[pl.ds(start, nk), cols]
        p, m, l = _masked_softmax(_dot_nt(qs, k2), mask)
        o2 = jnp.dot(p.astype(BF16), v2, preferred_element_type=F32)
        lse = m + jnp.log(jnp.maximum(l, 1e-30))
        o_ref[:, cols] = jnp.where(low, o2[:tq], o2[tq:])
        lse_ref[:, cols] = jnp.where(low, jnp.broadcast_to(lse[:tq], (tq, LANES)),
                                     jnp.broadcast_to(lse[tq:], (tq, LANES)))


def _dil_attn(hb, g, b, t, span, r, tq=128):
    ls = t // r
    n = hb.shape[1]
    per = n // DIL_GW
    hv = hb.reshape(b, ls, r * n)
    kern = functools.partial(_dil_attn_kernel, tq=tq, win=span // r, ls=ls)
    o, lse = pl.pallas_call(
        kern,
        grid=(b, r, ls // tq),
        in_specs=[pl.BlockSpec((None, tq, DIL_GW), lambda bi, rho, i: (bi, i, rho * per + 3 * g)),
                  pl.BlockSpec((None, ls, DIL_GW), lambda bi, rho, i: (bi, 0, rho * per + 3 * g + 1)),
                  pl.BlockSpec((None, ls, DIL_GW), lambda bi, rho, i: (bi, 0, rho * per + 3 * g + 2))],
        out_specs=[pl.BlockSpec((None, tq, DIL_GW), lambda bi, rho, i: (bi, i, rho)),
                   pl.BlockSpec((None, tq, DIL_GW), lambda bi, rho, i: (bi, i, rho))],
        out_shape=[jax.ShapeDtypeStruct((b, ls, r * DIL_GW), F32), jax.ShapeDtypeStruct((b, ls, r * DIL_GW), F32)],
        compiler_params=_cparams(("parallel", "parallel", "parallel")),
        name=f"dil_attn_{span}",
    )(hv, hv, hv)
    return o.reshape(b * t, DIL_GW), lse.reshape(b * t, DIL_GW)


def _dil_out_kernel(o0_ref, o1_ref, o2_ref, l0_ref, l1_ref, l2_ref, w_ref, r_ref, y_ref):
    ls = [l0_ref[...], l1_ref[...], l2_ref[...]]
    mx = jnp.maximum(jnp.maximum(ls[0], ls[1]), ls[2])
    es = [jnp.exp(l - mx) for l in ls]
    den = es[0] + es[1] + es[2]
    o = (es[0] / den) * o0_ref[...] + (es[1] / den) * o1_ref[...] + (es[2] / den) * o2_ref[...]
    y_ref[...] = r_ref[...] + jnp.dot(o.astype(BF16), w_ref[...], preferred_element_type=F32)


def _dil_out(os, lses, w, res, tm):
    m, d = res.shape
    row = lambda i: (i, 0)
    return pl.pallas_call(
        _dil_out_kernel,
        grid=(m // tm,),
        in_specs=[pl.BlockSpec((tm, DIL_GW), row)] * 6 + [pl.BlockSpec(w.shape, lambda i: (0, 0)),
                                                         pl.BlockSpec((tm, d), row)],
        out_specs=pl.BlockSpec((tm, d), row),
        out_shape=jax.ShapeDtypeStruct((m, d), F32),
        compiler_params=_cparams(("parallel",)),
        name="dil_out",
    )(*os, *lses, w, res)


MLA_QCW = MLA_HEADS * LANES
MLA_INP = 768
MLA_ST = MLA_KV_RANK + MLA_ROPE


def _mla_weights(w_in, w_uq, w_ukv):
    d = w_in.shape[0]
    n_in = MLA_Q_RANK + MLA_KV_RANK + MLA_ROPE
    win = jnp.concatenate([w_in, jnp.zeros((d, MLA_INP - n_in), w_in.dtype)], axis=1)
    uq = w_uq.reshape(MLA_Q_RANK, MLA_HEADS, MLA_NOPE + MLA_ROPE)
    uq = jnp.concatenate([uq, jnp.zeros((MLA_Q_RANK, MLA_HEADS, LANES - MLA_NOPE - MLA_ROPE), uq.dtype)], axis=2)
    ukv = w_ukv.reshape(MLA_KV_RANK, MLA_HEADS, MLA_NOPE + MLA_V)
    wk = jnp.concatenate([ukv[..., :MLA_NOPE], jnp.zeros((MLA_KV_RANK, MLA_HEADS, LANES - MLA_NOPE), ukv.dtype)],
                         axis=2)
    wv = ukv[..., MLA_NOPE:]
    return (win.astype(BF16), uq.reshape(MLA_Q_RANK, MLA_QCW).astype(BF16),
            wk.reshape(MLA_KV_RANK, MLA_QCW).astype(BF16), wv.reshape(MLA_KV_RANK, MLA_HEADS * MLA_V).astype(BF16))


def _mla_proj_kernel(x_ref, g_ref, win_ref, qg_ref, kvg_ref, wuq_ref, wk_ref, wv_ref,
                     qc_ref, qa_ref, qb_ref, kc_ref, ka_ref, kb_ref, q_ref, k_ref, v_ref, st_ref):
    xn = _rms(x_ref[...], g_ref[...]).astype(BF16)
    h = jnp.dot(xn, win_ref[...], preferred_element_type=F32)
    cq = _rms(h[:, 0:MLA_Q_RANK], qg_ref[...]).astype(BF16)
    ckv = _rms(h[:, MLA_Q_RANK:MLA_Q_RANK + MLA_KV_RANK], kvg_ref[...])
    half = MLA_ROPE // 2
    kpe = _rope(h[:, MLA_Q_RANK + MLA_KV_RANK:MLA_INP], kc_ref[...], ka_ref[...], kb_ref[...], half)
    st_ref[:, 0:MLA_KV_RANK] = ckv
    st_ref[:, MLA_KV_RANK:MLA_ST] = kpe[:, 0:MLA_ROPE]
    kpe_hi = pltpu.roll(kpe, MLA_NOPE, 1)
    ckv_b = ckv.astype(BF16)
    qc, qa, qb = qc_ref[...], qa_ref[...], qb_ref[...]
    for c in range(0, MLA_HEADS, 2):
        cols = slice(c * LANES, (c + 2) * LANES)
        qh = jnp.dot(cq, wuq_ref[:, cols], preferred_element_type=F32)
        kh = jnp.dot(ckv_b, wk_ref[:, cols], preferred_element_type=F32)
        for u in range(2):
            one = slice((c + u) * LANES, (c + u + 1) * LANES)
            q_ref[:, one] = _rope(qh[:, u * LANES:(u + 1) * LANES], qc, qa, qb, half).astype(BF16)
            k_ref[:, one] = (kh[:, u * LANES:(u + 1) * LANES] + kpe_hi).astype(BF16)
    v_ref[...] = jnp.dot(ckv_b, wv_ref[...], preferred_element_type=F32).astype(BF16)


def _mla_proj(x, p, ws, qtabs, ktabs, tab_blocks, tm):
    m, d = x.shape
    win, wuq, wk, wv = ws
    row = lambda i: (i, 0)
    tab = lambda i: (i % tab_blocks, 0)
    const = lambda i: (0, 0)
    full = lambda a: pl.BlockSpec(a.shape, const)
    qg = p['q_norm'].reshape(1, -1)
    kvg = p['kv_norm'].reshape(1, -1)
    return pl.pallas_call(
        _mla_proj_kernel,
        grid=(m // tm,),
        in_specs=[pl.BlockSpec((tm, d), row), pl.BlockSpec((1, d), const), full(win), full(qg), full(kvg),
                  full(wuq), full(wk), full(wv)] + [pl.BlockSpec((tm, LANES), tab)] * 6,
        out_specs=[pl.BlockSpec((tm, MLA_QCW), row), pl.BlockSpec((tm, MLA_QCW), row),
                   pl.BlockSpec((tm, MLA_HEADS * MLA_V), row), pl.BlockSpec((tm, MLA_ST), row)],
        out_shape=[jax.ShapeDtypeStruct((m, MLA_QCW), BF16), jax.ShapeDtypeStruct((m, MLA_QCW), BF16),
                   jax.ShapeDtypeStruct((m, MLA_HEADS * MLA_V), BF16), jax.ShapeDtypeStruct((m, MLA_ST), F32)],
        compiler_params=_cparams(("parallel",)),
        name="mla_proj",
    )(x, p['norm_attn'].reshape(1, d), win, qg, kvg, wuq, wk, wv, *qtabs, *ktabs)


def _mla_attn_kernel(q_ref, k_ref, v_ref, o_ref, m_ref, l_ref, acc_ref, *, tq, tk):
    i = pl.program_id(2)
    q0 = i * tq
    n_kt = (q0 + tq + tk - 1) // tk
    qp = q0 + lax.broadcasted_iota(jnp.int32, (tq, tk), 0)
    col = lax.broadcasted_iota(jnp.int32, (tq, tk), 1)
    low = lax.broadcasted_iota(jnp.int32, (1, LANES), 1) < MLA_V
    _online_init(m_ref, l_ref, acc_ref)
    for hh in range(2):
        rows = slice(hh * tq, (hh + 1) * tq)
        cols = slice(hh * LANES, (hh + 1) * LANES)

        def body(t, carry, rows=rows, cols=cols):
            k0 = pl.multiple_of(t * tk, tk)
            s = _dot_nt(q_ref[:, cols], k_ref[pl.ds(k0, tk), cols]) * MLA_SCALE
            s = jnp.where((k0 + col) <= qp, s, NEG_INF)
            _online_update(s, v_ref[pl.ds(k0, tk), :], m_ref.at[rows], l_ref.at[rows], acc_ref.at[rows])
            return carry

        lax.fori_loop(0, n_kt, body, 0)
    o = acc_ref[...] / jnp.maximum(l_ref[...], 1e-30)
    o_ref[...] = jnp.where(low, o[:tq], o[tq:]).astype(BF16)


def _mla_attn(q, k, v, b, t, tq=256, tk=256):
    tq, tk = min(tq, t), min(tk, t)
    kern = functools.partial(_mla_attn_kernel, tq=tq, tk=tk)
    return pl.pallas_call(
        kern,
        grid=(b, MLA_HEADS // 2, t // tq),
        in_specs=[pl.BlockSpec((None, tq, 2 * LANES), lambda bi, hp, i: (bi, i, hp)),
                  pl.BlockSpec((None, t, 2 * LANES), lambda bi, hp, i: (bi, 0, hp)),
                  pl.BlockSpec((None, t, LANES), lambda bi, hp, i: (bi, 0, hp))],
        out_specs=pl.BlockSpec((None, tq, LANES), lambda bi, hp, i: (bi, i, hp)),
        out_shape=jax.ShapeDtypeStruct((b, t, MLA_HEADS * MLA_V), BF16),
        scratch_shapes=[pltpu.VMEM((2 * tq, 1), F32), pltpu.VMEM((2 * tq, 1), F32),
                        pltpu.VMEM((2 * tq, LANES), F32)],
        compiler_params=_cparams(("parallel", "parallel", "parallel")),
        name="mla_attn",
    )(q.reshape(b, t, MLA_QCW), k.reshape(b, t, MLA_QCW), v.reshape(b, t, MLA_HEADS * MLA_V))


def _round_up(x, m):
    return (x + m - 1) // m * m


def _pool_feature_major(pool):
    n_phys, page = pool.shape[0], pool.shape[1]
    return jnp.transpose(pool, (0, 2, 3, 4, 1)).reshape(n_phys, 4 * NSA_KV * HEAD_DIM, page)


def _shift_rows_left(src_ref, dst_ref, new_row):
    f, w = src_ref.shape
    nt = w // LANES
    lane = lax.broadcasted_iota(jnp.int32, (f, LANES), 1)
    new_col = jnp.transpose(jnp.broadcast_to(new_row, (LANES, f)))
    nxt = pltpu.roll(src_ref[:, 0:LANES], LANES - 1, 1)
    for j in range(nt):
        cur = nxt
        if j + 1 < nt:
            nxt = pltpu.roll(src_ref[:, (j + 1) * LANES:(j + 2) * LANES], LANES - 1, 1)
            fill = nxt
        else:
            fill = new_col
        dst_ref[:, j * LANES:(j + 1) * LANES] = jnp.where(lane < LANES - 1, cur, fill)


def _stack_heads(q8):
    low = lax.broadcasted_iota(jnp.int32, (1, LANES), 1) < HEAD_DIM
    qf = q8.astype(F32)
    zero = jnp.zeros_like(qf)
    return jnp.concatenate([jnp.where(low, qf, zero), jnp.where(low, zero, qf)], axis=0).astype(BF16)


def _nsa_dec_cmp_kernel(pt_ref, q_ref, new_ref, pool_ref, w_ref, proj_ref, smat_ref, tri_ref,
                        oc_ref, idx_ref, cbuf, f_ref, s_ref, sem, *, n_pages, page, past, n_sel):
    b = pl.program_id(0)
    nb = pl.num_programs(0)
    slot = b % 2
    width = 2 * LANES
    new_rows = SLC_BLOCK
    n_chunks = (past + new_rows) // CMP_STRIDE
    n_cmp = n_chunks - 1
    ncp = f_ref.shape[0]

    def fetch(bb, sl):
        def body(pg, c):
            return pltpu.make_async_copy(pool_ref.at[pt_ref[bb, pg], pl.ds(0, width), :],
                                         cbuf.at[sl, pg], sem.at[sl])
        return body

    def start(bb, sl):
        mk = fetch(bb, sl)

        def body(pg, c):
            mk(pg, c).start()
            return c
        lax.fori_loop(0, n_pages, body, 0)

    @pl.when(b == 0)
    def _():
        start(0, 0)

    @pl.when(b + 1 < nb)
    def _():
        start(b + 1, 1 - slot)

    mk = fetch(b, slot)

    def wait_body(pg, c):
        mk(pg, c).wait()
        return c
    lax.fori_loop(0, n_pages, wait_body, 0)

    w = w_ref[...]
    per = page // CMP_STRIDE

    def sum_body(pg, carry):
        first, second = _chunk_sums(cbuf[slot, pg].T, w)
        f_ref[pl.ds(pl.multiple_of(pg * per, per), per), :] = first
        s_ref[pl.ds(pl.multiple_of(pg * per, per), per), :] = second
        return carry
    lax.fori_loop(0, n_pages, sum_body, 0)
    tail0 = past // CMP_STRIDE
    f_ref[tail0:ncp, :] = jnp.zeros((ncp - tail0, width), F32)
    s_ref[tail0:ncp, :] = jnp.zeros((ncp - tail0, width), F32)
    ridx = lax.broadcasted_iota(jnp.int32, (new_rows, width), 0)
    first, second = _chunk_sums(jnp.where(ridx == 0, new_ref[:, 0:width], 0.0), w)
    f_ref[tail0:tail0 + new_rows // CMP_STRIDE, :] = first
    s_ref[tail0:tail0 + new_rows // CMP_STRIDE, :] = second

    nxt = pltpu.roll(s_ref[...], ncp - 1, 0)
    rid = lax.broadcasted_iota(jnp.int32, (ncp, width), 0)
    ssum = jnp.where(rid < n_cmp, f_ref[...] + nxt, 0.0)
    ckv = jnp.dot(ssum.astype(BF16), proj_ref[...], preferred_element_type=F32).astype(BF16)

    qs = _stack_heads(q_ref[...])
    s = _dot_nt(qs, ckv[:, 0:LANES])
    col = lax.broadcasted_iota(jnp.int32, (NSA_HEADS, ncp), 1)
    p, _, _ = _masked_softmax(s, (col * CMP_STRIDE + CMP_BLOCK - 1) <= past)
    oc_ref[...] = jnp.dot(p.astype(BF16), ckv[:, LANES:width], preferred_element_type=F32)

    imp = jnp.sum(p.reshape(NSA_KV, NSA_HPG, ncp), axis=1)
    row8 = lax.broadcasted_iota(jnp.int32, (8, ncp), 0)
    imp8 = jnp.where(row8 == 0, imp[0:1], jnp.where(row8 == 1, imp[1:2], 0.0))
    sc = _dot3(imp8, smat_ref[...])
    nl = sc.shape[1]
    blk = lax.broadcasted_iota(jnp.int32, (8, nl), 1)
    cur = past // SLC_BLOCK
    forced = (blk == 0) | (blk == cur) | (blk == cur - 1)
    future = (blk * SLC_BLOCK) > past
    sc = jnp.where(forced, jnp.inf, jnp.where(future, NEG_INF, sc))
    sc = jnp.where(blk < n_sel, sc, NEG_INF)
    k = min(SLC_TOPK, n_sel)
    sel = _topk_mask(sc, n_sel, k)
    cnt = jnp.dot(sel.astype(BF16), tri_ref[...], preferred_element_type=F32)
    slot_id = lax.broadcasted_iota(jnp.int32, (SLC_TOPK, nl), 0).astype(F32)
    blk_f = lax.broadcasted_iota(jnp.int32, (SLC_TOPK, nl), 1).astype(F32)
    for g in range(NSA_KV):
        hit = (sel[g:g + 1] > 0.5) & (cnt[g:g + 1] == slot_id)
        ids = jnp.sum(jnp.where(hit, blk_f, 0.0), axis=-1, keepdims=True)
        idx_ref[g * SLC_TOPK:(g + 1) * SLC_TOPK, :] = jnp.broadcast_to(ids, (SLC_TOPK, LANES)).astype(jnp.int32)


def _nsa_dec_cmp(q8, kvf, pool, page_table, cw, cproj):
    bd, n_pages = page_table.shape
    n_phys, page = pool.shape[0], pool.shape[1]
    past = n_pages * page
    n_chunks = (past + SLC_BLOCK) // CMP_STRIDE
    ncp = _round_up(n_chunks, LANES)
    n_sel = past // SLC_BLOCK + 1
    assert n_sel <= 2 * LANES and SLC_TOPK <= n_sel
    smat = _select_matrix(ncp, n_sel)
    nl = smat.shape[1]
    tri = jnp.asarray(np.triu(np.ones((nl, nl)), 1), BF16)
    pool3 = _pool_feature_major(pool)
    kern = functools.partial(_nsa_dec_cmp_kernel, n_pages=n_pages, page=page, past=past, n_sel=n_sel)
    const = lambda i, pt: (0, 0)
    grid_spec = pltpu.PrefetchScalarGridSpec(
        num_scalar_prefetch=1,
        grid=(bd,),
        in_specs=[pl.BlockSpec((None, NSA_HPG, LANES), lambda i, pt: (i, 0, 0)),
                  pl.BlockSpec((None, 1, NSA_KVW), lambda i, pt: (i, 0, 0)),
                  pl.BlockSpec(memory_space=pl.ANY),
                  pl.BlockSpec(cw.shape, const), pl.BlockSpec(cproj.shape, const),
                  pl.BlockSpec(smat.shape, const), pl.BlockSpec(tri.shape, const)],
        out_specs=[pl.BlockSpec((None, NSA_HEADS, LANES), lambda i, pt: (i, 0, 0)),
                   pl.BlockSpec((None, NSA_KV * SLC_TOPK, LANES), lambda i, pt: (i, 0, 0))],
        scratch_shapes=[pltpu.VMEM((2, n_pages, 2 * LANES, page), F32),
                        pltpu.VMEM((ncp, 2 * LANES), F32), pltpu.VMEM((ncp, 2 * LANES), F32),
                        pltpu.SemaphoreType.DMA((2,))])
    return pl.pallas_call(
        kern,
        grid_spec=grid_spec,
        out_shape=[jax.ShapeDtypeStruct((bd, NSA_HEADS, LANES), F32),
                   jax.ShapeDtypeStruct((bd, NSA_KV * SLC_TOPK, LANES), jnp.int32)],
        compiler_params=_cparams(("arbitrary",)),
        name="nsa_dec_cmp",
    )(page_table, q8, kvf.reshape(bd, 1, NSA_KVW), pool3, cw, cproj, smat, tri)


def _nsa_dec_sel_kernel(pt_ref, ids_ref, q_ref, new_ref, gate_ref, oc_ref, idl_ref, win_ref, pool_ref,
                        o_ref, wout_ref, sbuf, sem, *, past, wb):
    b = pl.program_id(0)
    nb = pl.num_programs(0)
    slot = b % 2
    width = 2 * LANES
    n_slots = NSA_KV * SLC_TOPK
    n_past = past // SLC_BLOCK
    page = pool_ref.shape[2]
    sub = page // SLC_BLOCK

    def copy(bb, sl, s):
        ip = jnp.minimum(ids_ref[bb, s], n_past - 1)
        pg = pt_ref[bb, ip // sub]
        return pltpu.make_async_copy(pool_ref.at[pg, pl.ds(width, width), :], sbuf.at[sl, s], sem.at[sl])

    def start(bb, sl):
        def body(s, c):
            copy(bb, sl, s).start()
            return c
        lax.fori_loop(0, n_slots, body, 0)

    @pl.when(b == 0)
    def _():
        start(0, 0)

    @pl.when(b + 1 < nb)
    def _():
        start(b + 1, 1 - slot)

    def wait_body(s, c):
        copy(b, slot, s).wait()
        return c
    lax.fori_loop(0, n_slots, wait_body, 0)

    qs = _stack_heads(q_ref[...])
    new = new_ref[...]
    nk = SLC_TOPK * page
    low = lax.broadcasted_iota(jnp.int32, (1, LANES), 1) < HEAD_DIM

    new_sb = new[:, width:2 * width].astype(BF16).astype(F32)
    ecol = jnp.where(lax.broadcasted_iota(jnp.int32, (LANES, nk), 0)
                     == _div_p2(lax.broadcasted_iota(jnp.int32, (LANES, nk), 1), page), 1.0, 0.0).astype(BF16)
    blk_in_page = _div_p2(_mod_p2(lax.broadcasted_iota(jnp.int32, (8, nk), 1), page), SLC_BLOCK)
    o_sel = []
    for g in range(NSA_KV):
        qg = qs[g * NSA_HPG:(g + 1) * NSA_HPG]
        idl = idl_ref[:, g * LANES:(g + 1) * LANES]
        blk_l = jnp.dot(idl.astype(BF16), ecol, preferred_element_type=F32).astype(jnp.int32)
        from_pool = blk_l < n_past
        valid = from_pool & (_mod_p2(blk_l, sub) == blk_in_page)
        has_new = jnp.max(jnp.where(from_pool, 0.0, 1.0), axis=-1, keepdims=True) > 0.5
        kt = jnp.concatenate([sbuf[slot, g * SLC_TOPK + k, 0:LANES, :] for k in range(SLC_TOPK)],
                             axis=1).astype(BF16)
        vt = jnp.concatenate([sbuf[slot, g * SLC_TOPK + k, LANES:width, :] for k in range(SLC_TOPK)],
                             axis=1).astype(BF16)
        s = jnp.where(valid, jnp.dot(qg, kt, preferred_element_type=F32), NEG_INF)
        s_n = jnp.sum(qg.astype(F32) * new_sb[:, 0:LANES], axis=-1, keepdims=True)
        s_n = jnp.where(has_new, s_n, NEG_INF)
        m = jnp.maximum(jnp.max(s, axis=-1, keepdims=True), s_n)
        m = jnp.where(m > NEG_INF, m, 0.0)
        e = jnp.exp(s - m)
        e_n = jnp.exp(s_n - m)
        l = jnp.maximum(jnp.sum(e, axis=-1, keepdims=True) + e_n, 1e-30)
        o_sel.append(_dot_nt((e / l).astype(BF16), vt)
                     + (e_n / l).astype(BF16).astype(F32) * new_sb[:, LANES:width])
    os_ = jnp.concatenate(o_sel, axis=0)

    win = win_ref[...]
    winb = win.astype(BF16)
    new_win = new[:, 2 * width:3 * width]
    new_wb = new_win.astype(BF16).astype(F32)
    s_w = jnp.dot(qs, winb[0:LANES], preferred_element_type=F32)
    s_n = jnp.sum(qs.astype(F32) * new_wb[:, 0:LANES], axis=-1, keepdims=True)
    kidx = lax.broadcasted_iota(jnp.int32, (NSA_HEADS, wb), 1)
    dist = wb - kidx
    s_w = jnp.where(dist <= NSA_WINDOW, s_w, NEG_INF)
    m = jnp.maximum(jnp.max(s_w, axis=-1, keepdims=True), s_n)
    e_w = jnp.exp(s_w - m)
    e_n = jnp.exp(s_n - m)
    l = jnp.maximum(jnp.sum(e_w, axis=-1, keepdims=True) + e_n, 1e-30)
    ow = (_dot_nt((e_w / l).astype(BF16), winb[LANES:width])
          + (e_n / l).astype(BF16).astype(F32) * new_wb[:, LANES:width])

    gl = lax.broadcasted_iota(jnp.int32, (NSA_HEADS, LANES), 1)
    gh = lax.broadcasted_iota(jnp.int32, (NSA_HEADS, LANES), 0)
    grow = jnp.broadcast_to(gate_ref[...], (NSA_HEADS, LANES))
    gcol = lambda br: jnp.sum(jnp.where(gl == br * NSA_HEADS + gh, grow, 0.0), axis=-1, keepdims=True)
    x = gcol(0) * oc_ref[...] + gcol(1) * os_ + gcol(2) * ow
    o_ref[...] = jnp.where(low, x[0:NSA_HPG], x[NSA_HPG:]).astype(BF16)

    _shift_rows_left(win_ref, wout_ref, new_win)


def _nsa_dec_sel(q8, kvf, gates, oc, ids, pool, win_buf, page_table):
    bd, n_pages = page_table.shape
    n_phys, page = pool.shape[0], pool.shape[1]
    past = n_pages * page
    wb = win_buf.shape[1]
    ids2 = ids[:, :, 0]
    idl = ids2.astype(F32).reshape(bd, NSA_KV, SLC_TOPK)
    idl = jnp.pad(idl, ((0, 0), (0, 0), (0, LANES - SLC_TOPK))).reshape(bd, 1, NSA_KV * LANES)
    idl = jnp.broadcast_to(idl, (bd, 8, NSA_KV * LANES))
    assert wb % LANES == 0 and page % SLC_BLOCK == 0
    pool3 = _pool_feature_major(pool)
    win3 = jnp.transpose(win_buf, (0, 2, 3, 4, 1)).reshape(bd, 2 * LANES, wb)
    kern = functools.partial(_nsa_dec_sel_kernel, past=past, wb=wb)
    row3 = lambda i, pt, sid: (i, 0, 0)
    grid_spec = pltpu.PrefetchScalarGridSpec(
        num_scalar_prefetch=2,
        grid=(bd,),
        in_specs=[pl.BlockSpec((None, NSA_HPG, LANES), row3), pl.BlockSpec((None, 1, NSA_KVW), row3),
                  pl.BlockSpec((None, 1, LANES), row3), pl.BlockSpec((None, NSA_HEADS, LANES), row3),
                  pl.BlockSpec((None, 8, NSA_KV * LANES), row3), pl.BlockSpec((None, 2 * LANES, wb), row3),
                  pl.BlockSpec(memory_space=pl.ANY)],
        out_specs=[pl.BlockSpec((None, NSA_HPG, LANES), row3), pl.BlockSpec((None, 2 * LANES, wb), row3)],
        scratch_shapes=[pltpu.VMEM((2, NSA_KV * SLC_TOPK, 2 * LANES, page), F32),
                        pltpu.SemaphoreType.DMA((2,))])
    o, wout = pl.pallas_call(
        kern,
        grid_spec=grid_spec,
        out_shape=[jax.ShapeDtypeStruct((bd, NSA_HPG, LANES), BF16),
                   jax.ShapeDtypeStruct((bd, 2 * LANES, wb), F32)],
        compiler_params=_cparams(("arbitrary",)),
        name="nsa_dec_sel",
    )(page_table, ids2, q8, kvf.reshape(bd, 1, NSA_KVW), gates.reshape(bd, 1, LANES), oc, idl, win3, pool3)
    wout = jnp.transpose(wout.reshape(bd, 2, NSA_KV, HEAD_DIM, wb), (0, 4, 1, 2, 3))
    return o, wout


def _dil_dec_kernel(hq_ref, st_ref, c0_ref, c1_ref, c2_ref, o_ref, s0_ref, s1_ref, s2_ref):
    c = pl.program_id(1)
    caches = (c0_ref, c1_ref, c2_ref)
    outs = (s0_ref, s1_ref, s2_ref)
    tiles = DIL_GW // LANES
    low = lax.broadcasted_iota(jnp.int32, (1, LANES), 1) < HEAD_DIM
    row = lax.broadcasted_iota(jnp.int32, (8, LANES), 0)
    lane8 = lax.broadcasted_iota(jnp.int32, (8, LANES), 1)
    mine = ((row == 0) & (lane8 < HEAD_DIM)) | ((row == 1) & (lane8 >= HEAD_DIM))
    o_g, lse_g = [], []
    for g, (span, r) in enumerate(DIL_PAIRS):
        wb = caches[g].shape[2]
        q = hq_ref[3 * g * tiles + c].astype(F32)
        kn = st_ref[2 * g * tiles + c]
        vn = st_ref[(2 * g + 1) * tiles + c]
        q2 = jnp.where(mine, jnp.broadcast_to(q, (8, LANES)), 0.0)
        knb = kn.astype(BF16).astype(F32)
        vnb = vn.astype(BF16).astype(F32)
        kt = caches[g][0].astype(BF16)
        vt = caches[g][1].astype(BF16)
        s = jnp.dot(q2.astype(BF16), kt, preferred_element_type=F32)
        t = lax.broadcasted_iota(jnp.int32, (8, wb), 1)
        s = jnp.where(_mod_p2(t, r) == 0, s, NEG_INF)
        s_n = jnp.sum(q2 * knb, axis=-1, keepdims=True)
        m = jnp.maximum(jnp.max(s, axis=-1, keepdims=True), s_n)
        e = jnp.exp(s - m)
        e_n = jnp.exp(s_n - m)
        l = jnp.maximum(jnp.sum(e, axis=-1, keepdims=True) + e_n, 1e-30)
        o_g.append(_dot_nt((e / l).astype(BF16), vt) + (e_n / l).astype(BF16).astype(F32) * vnb)
        lse_g.append(m + jnp.log(l))
        _shift_rows_left(caches[g].at[0], outs[g].at[0], kn)
        _shift_rows_left(caches[g].at[1], outs[g].at[1], vn)
    mx = jnp.maximum(jnp.maximum(lse_g[0], lse_g[1]), lse_g[2])
    es = [jnp.exp(x - mx) for x in lse_g]
    den = es[0] + es[1] + es[2]
    o = (es[0] / den) * o_g[0] + (es[1] / den) * o_g[1] + (es[2] / den) * o_g[2]
    o_ref[...] = jnp.where(low, o[0:1], o[1:2]).astype(BF16)


def _dil_dec(hb, st, bufs):
    bd = hb.shape[0]
    tiles = DIL_GW // LANES
    views, specs, shapes = [], [], []
    for (span, r), buf in zip(DIL_PAIRS, bufs):
        wb = buf.shape[1]
        assert wb == span and wb % r == 0 and wb % LANES == 0
        views.append(jnp.transpose(buf, (0, 2, 3, 4, 1)).reshape(bd, 2, tiles, LANES, wb))
        specs.append(pl.BlockSpec((None, 2, None, LANES, wb), lambda i, c: (i, 0, c, 0, 0)))
        shapes.append(jax.ShapeDtypeStruct((bd, 2, tiles, LANES, wb), F32))
    nq = hb.shape[1] // LANES
    ns = st.shape[1] // LANES
    res = pl.pallas_call(
        _dil_dec_kernel,
        grid=(bd, tiles),
        in_specs=[pl.BlockSpec((None, nq, 1, LANES), lambda i, c: (i, 0, 0, 0)),
                  pl.BlockSpec((None, ns, 1, LANES), lambda i, c: (i, 0, 0, 0))] + specs,
        out_specs=[pl.BlockSpec((None, None, 1, LANES), lambda i, c: (i, c, 0, 0))] + specs,
        out_shape=[jax.ShapeDtypeStruct((bd, tiles, 1, LANES), BF16)] + shapes,
        compiler_params=_cparams(("parallel", "parallel")),
        name="dil_dec",
    )(hb.reshape(bd, nq, 1, LANES), st.reshape(bd, ns, 1, LANES), *views)
    states = [jnp.transpose(s.reshape(bd, 2, DIL_HEADS, HEAD_DIM, s.shape[-1]), (0, 4, 1, 2, 3)) for s in res[1:]]
    return res[0].reshape(bd, DIL_GW), states


def _mla_qlat_kernel(q_ref, wkt_ref, ql_ref, qp_ref):
    for h in range(MLA_HEADS):
        qh = q_ref[:, h * LANES:(h + 1) * LANES]
        ql_ref[h] = jnp.dot(qh, wkt_ref[h], preferred_element_type=F32).astype(BF16)
        lane = lax.broadcasted_iota(jnp.int32, qh.shape, 1)
        qpe = jnp.where((lane >= MLA_NOPE) & (lane < MLA_NOPE + MLA_ROPE), qh.astype(F32), 0.0)
        qp_ref[h] = pltpu.roll(qpe, LANES - MLA_NOPE - MLA_ROPE, 1).astype(BF16)


def _mla_qlat(q, wkt):
    bd = q.shape[0]
    return pl.pallas_call(
        _mla_qlat_kernel,
        out_shape=[jax.ShapeDtypeStruct((MLA_HEADS, bd, MLA_KV_RANK), BF16),
                   jax.ShapeDtypeStruct((MLA_HEADS, bd, LANES), BF16)],
        compiler_params=pltpu.CompilerParams(vmem_limit_bytes=VMEM_LIMIT),
        name="mla_qlat",
    )(q, wkt)


def _mla_dec_kernel(pt_ref, ql_ref, qp_ref, new_ref, pool_ref, o_ref, buf, m_ref, l_ref, acc_ref, sem,
                    *, n_pages, page, past, chunk):
    b = pl.program_id(0)
    nb = pl.num_programs(0)
    slot = b % 2

    def fetch(bb, sl):
        def mk(pg):
            return pltpu.make_async_copy(pool_ref.at[pt_ref[bb, pg]], buf.at[sl, pg], sem.at[sl])
        return mk

    def start(bb, sl):
        mk = fetch(bb, sl)

        def body(pg, c):
            mk(pg).start()
            return c
        lax.fori_loop(0, n_pages, body, 0)

    @pl.when(b == 0)
    def _():
        start(0, 0)

    @pl.when(b + 1 < nb)
    def _():
        start(b + 1, 1 - slot)

    mk = fetch(b, slot)

    def wait_body(pg, c):
        mk(pg).wait()
        return c
    lax.fori_loop(0, n_pages, wait_body, 0)

    ql = ql_ref[...]
    qp = qp_ref[...]
    pe0 = MLA_ST - LANES
    _online_init(m_ref, l_ref, acc_ref)
    ppc = chunk // page

    def body(c, carry):
        x = jnp.concatenate([buf[slot, c * ppc + u] for u in range(ppc)], axis=1).astype(BF16)
        lat = x[0:MLA_KV_RANK]
        s = (jnp.dot(ql, lat, preferred_element_type=F32)
             + jnp.dot(qp, x[pe0:MLA_ST], preferred_element_type=F32)) * MLA_SCALE
        m_old = m_ref[...]
        m_new = jnp.maximum(m_old, jnp.max(s, axis=-1, keepdims=True))
        alpha = jnp.exp(m_old - m_new)
        p = jnp.exp(s - m_new)
        l_ref[...] = alpha * l_ref[...] + jnp.sum(p, axis=-1, keepdims=True)
        acc_ref[...] = alpha * acc_ref[...] + _dot_nt(p.astype(BF16), lat)
        m_ref[...] = m_new
        return carry
    lax.fori_loop(0, past // chunk, body, 0)

    newb = new_ref[...].astype(BF16).astype(F32)
    s_n = (jnp.sum(ql.astype(F32) * newb[:, 0:MLA_KV_RANK], axis=-1, keepdims=True)
           + jnp.sum(qp.astype(F32)[:, LANES - MLA_ROPE:] * newb[:, MLA_KV_RANK:MLA_ST], axis=-1, keepdims=True)
           ) * MLA_SCALE
    m_old = m_ref[...]
    m_new = jnp.maximum(m_old, s_n)
    alpha = jnp.exp(m_old - m_new)
    p_n = jnp.exp(s_n - m_new)
    l = jnp.maximum(alpha * l_ref[...] + p_n, 1e-30)
    acc = alpha * acc_ref[...] + p_n.astype(BF16).astype(F32) * newb[:, 0:MLA_KV_RANK]
    o_ref[...] = acc / l


def _mla_dec(ql, qp, st, pool, page_table, chunk=1024):
    bd, n_pages = page_table.shape
    page = pool.shape[1]
    past = n_pages * page
    chunk = min(chunk, past)
    kern = functools.partial(_mla_dec_kernel, n_pages=n_pages, page=page, past=past, chunk=chunk)
    row3 = lambda i, pt: (i, 0, 0)
    grid_spec = pltpu.PrefetchScalarGridSpec(
        num_scalar_prefetch=1,
        grid=(bd,),
        in_specs=[pl.BlockSpec((None, MLA_HEADS, MLA_KV_RANK), row3), pl.BlockSpec((None, MLA_HEADS, LANES), row3),
                  pl.BlockSpec((None, 1, MLA_ST), row3), pl.BlockSpec(memory_space=pl.ANY)],
        out_specs=pl.BlockSpec((None, MLA_HEADS, MLA_KV_RANK), row3),
        scratch_shapes=[pltpu.VMEM((2, n_pages, MLA_ST, page), F32), pltpu.VMEM((MLA_HEADS, 1), F32),
                        pltpu.VMEM((MLA_HEADS, 1), F32), pltpu.VMEM((MLA_HEADS, MLA_KV_RANK), F32),
                        pltpu.SemaphoreType.DMA((2,))])
    return pl.pallas_call(
        kern,
        grid_spec=grid_spec,
        out_shape=jax.ShapeDtypeStruct((bd, MLA_HEADS, MLA_KV_RANK), F32),
        compiler_params=_cparams(("arbitrary",)),
        name="mla_dec",
    )(page_table, ql, qp, st.reshape(bd, 1, MLA_ST), jnp.transpose(pool, (0, 2, 1)))


def _mla_dec_out_kernel(ol_ref, wvp_ref, w_ref, r_ref, y_ref, o_scr):
    for j in range(MLA_HEADS // 2):
        acc = jnp.dot(ol_ref[2 * j].astype(BF16), wvp_ref[2 * j], preferred_element_type=F32)
        acc = acc + jnp.dot(ol_ref[2 * j + 1].astype(BF16), wvp_ref[2 * j + 1], preferred_element_type=F32)
        o_scr[:, j * LANES:(j + 1) * LANES] = acc.astype(BF16)
    y_ref[...] = r_ref[...] + jnp.dot(o_scr[...], w_ref[...], preferred_element_type=F32)


def _mla_dec_out(ol, wvp, w_out, res):
    bd, d = res.shape
    return pl.pallas_call(
        _mla_dec_out_kernel,
        out_shape=jax.ShapeDtypeStruct((bd, d), F32),
        scratch_shapes=[pltpu.VMEM((bd, MLA_HEADS * MLA_V), BF16)],
        compiler_params=pltpu.CompilerParams(vmem_limit_bytes=VMEM_LIMIT),
        name="mla_dec_out",
    )(ol, wvp, w_out, res)


def _dil_sample_layer(h, p, bufs, past):
    bd = h.shape[0]
    tabs = _rope_tables(jnp.full((bd,), past), ROT_DIM, ROPE_THETA, HEAD_DIM, 0)
    hb, st = _dil_proj(h, p['norm_attn'], p['w_in'].astype(BF16), tabs, 1, bd)
    o, states = _dil_dec(hb, st, bufs)
    y = _out_proj(o, p['w_out'].astype(BF16), h, bd)
    return y, states


def _mla_sample_layer(h, p, pool, page_table):
    bd = h.shape[0]
    past = page_table.shape[1] * pool.shape[1]
    ws = _mla_weights(p['w_in'], p['w_uq'], p['w_ukv'])
    pos = jnp.full((bd,), past)
    qtabs = _rope_tables(pos, MLA_ROPE, MLA_THETA, LANES, MLA_NOPE)
    ktabs = _rope_tables(pos, MLA_ROPE, MLA_THETA, LANES, 0)
    q, _, _, st = _mla_proj(h, p, ws, qtabs, ktabs, 1, bd)
    ukv = p['w_ukv'].reshape(MLA_KV_RANK, MLA_HEADS, MLA_NOPE + MLA_V)
    wkt = jnp.transpose(ukv[..., :MLA_NOPE], (1, 2, 0))
    wkt = jnp.concatenate([wkt, jnp.zeros((MLA_HEADS, LANES - MLA_NOPE, MLA_KV_RANK), wkt.dtype)], axis=1)
    ql, qp = _mla_qlat(q, wkt.astype(BF16))
    ol = _mla_dec(jnp.transpose(ql, (1, 0, 2)), jnp.transpose(qp, (1, 0, 2)), st, pool, page_table)
    wv = jnp.transpose(ukv[..., MLA_NOPE:], (1, 0, 2))
    z = jnp.zeros_like(wv)
    even = (np.arange(MLA_HEADS) % 2 == 0)[:, None, None]
    wvp = jnp.concatenate([jnp.where(even, wv, z), jnp.where(even, z, wv)], axis=2)
    y = _mla_dec_out(jnp.transpose(ol, (1, 0, 2)), wvp.astype(BF16), p['w_out'].astype(BF16), h)
    return y, st.reshape(bd, 1, MLA_ST)


def _nsa_sample_layer(h, p, pool, win_buf, page_table):
    bd = h.shape[0]
    past = page_table.shape[1] * pool.shape[1]
    w_in, w_out = _nsa_weights(p['w_in'], p['w_out'])
    tabs = _rope_tables(jnp.full((bd,), past), ROT_DIM, ROPE_THETA, HEAD_DIM, 0)
    q, kvf, _, gates = _nsa_proj(h, p['norm_attn'], w_in, tabs, 1, bd)
    q8 = q.reshape(bd, NSA_HPG, LANES)
    cw, cproj = _cmp_weights(p['cmp_pos'], p['cmp_proj'])
    oc, ids = _nsa_dec_cmp(q8, kvf, pool, page_table, cw, cproj)
    o, wout = _nsa_dec_sel(q8, kvf, gates, oc, ids, pool, win_buf, page_table)
    y = _out_proj(o.reshape(bd, NSA_QW), w_out, h, bd)
    kv5 = kvf.reshape(bd, 1, 6, NSA_KV, HEAD_DIM)
    return y, kv5[:, :, 0:4], wout.reshape(win_buf.shape)


def _dil_prompt_layer(h, p, b, t, tm):
    w_in = p['w_in'].astype(BF16)
    tabs = _rope_tables(jnp.arange(t), ROT_DIM, ROPE_THETA, HEAD_DIM, 0)
    hb, st = _dil_proj(h, p['norm_attn'], w_in, tabs, t // tm, tm)
    os, lses, states = [], [], []
    for g, (span, r) in enumerate(DIL_PAIRS):
        o, lse = _dil_attn(hb, g, b, t, span, r)
        os.append(o)
        lses.append(lse)
        s = st.reshape(b, t, 2 * DIL_NG * DIL_GW)[:, t - min(span, t):, 2 * g * DIL_GW:2 * (g + 1) * DIL_GW]
        states.append(s.reshape(b, min(span, t), 2, DIL_HEADS, HEAD_DIM))
    y = _dil_out(os, lses, p['w_out'].astype(BF16), h, tm)
    return y, states


def _mla_prompt_layer(h, p, b, t, tm):
    ws = _mla_weights(p['w_in'], p['w_uq'], p['w_ukv'])
    pos = jnp.arange(t)
    qtabs = _rope_tables(pos, MLA_ROPE, MLA_THETA, LANES, MLA_NOPE)
    ktabs = _rope_tables(pos, MLA_ROPE, MLA_THETA, LANES, 0)
    q, k, v, st = _mla_proj(h, p, ws, qtabs, ktabs, t // tm, tm)
    o = _mla_attn(q, k, v, b, t)
    y = _out_proj(o.reshape(b * t, MLA_HEADS * MLA_V), p['w_out'].astype(BF16), h, tm)
    return y, st.reshape(b, t, MLA_ST)


def _nsa_prompt_layer(h, p, b, t, tm):
    w_in, w_out = _nsa_weights(p['w_in'], p['w_out'])
    tabs = _rope_tables(jnp.arange(t), ROT_DIM, ROPE_THETA, HEAD_DIM, 0)
    q, kvf, kvb, gates = _nsa_proj(h, p['norm_attn'], w_in, tabs, t // tm, tm)
    cw, cproj = _cmp_weights(p['cmp_pos'], p['cmp_proj'])
    ckv = _cmp_prompt(kvf, cw, cproj, b, t)
    o = _nsa_attn(q, kvb, ckv, gates, b, t)
    y = _out_proj(o.reshape(b * t, NSA_QW), w_out, h, tm)
    kv5 = kvf.reshape(b, t, 6, NSA_KV, HEAD_DIM)
    return y, kv5[:, :, 0:4], kv5[:, t - min(NSA_WINDOW, t):, 4:6]


def kernel(x_prompt, x_sample, cache_l0_nsa, cache_l0_win, cache_l1_dil_w128, cache_l1_dil_w512, cache_l1_dil_w2048, cache_l2_mla, cache_l3_nsa, cache_l3_win, page_table, l0_norm_attn, l0_w_in, l0_cmp_pos, l0_cmp_proj, l0_w_out, l0_norm_mlp, l0_w_up, l0_w_down, l1_norm_attn, l1_w_in, l1_w_out, l1_norm_mlp, l1_w_up, l1_w_down, l2_norm_attn, l2_w_in, l2_q_norm, l2_w_uq, l2_kv_norm, l2_w_ukv, l2_w_out, l2_norm_mlp, l2_w_up, l2_w_down, l3_norm_attn, l3_w_in, l3_cmp_pos, l3_cmp_proj, l3_w_out, l3_norm_mlp, l3_w_up, l3_w_down, final_norm):
    layers = [
        dict(norm_attn=l0_norm_attn, w_in=l0_w_in, cmp_pos=l0_cmp_pos, cmp_proj=l0_cmp_proj, w_out=l0_w_out,
             norm_mlp=l0_norm_mlp, w_up=l0_w_up, w_down=l0_w_down),
        dict(norm_attn=l1_norm_attn, w_in=l1_w_in, w_out=l1_w_out,
             norm_mlp=l1_norm_mlp, w_up=l1_w_up, w_down=l1_w_down),
        dict(norm_attn=l2_norm_attn, w_in=l2_w_in, q_norm=l2_q_norm, w_uq=l2_w_uq, kv_norm=l2_kv_norm,
             w_ukv=l2_w_ukv, w_out=l2_w_out, norm_mlp=l2_norm_mlp, w_up=l2_w_up, w_down=l2_w_down),
        dict(norm_attn=l3_norm_attn, w_in=l3_w_in, cmp_pos=l3_cmp_pos, cmp_proj=l3_cmp_proj, w_out=l3_w_out,
             norm_mlp=l3_norm_mlp, w_up=l3_w_up, w_down=l3_w_down),
    ]
    caches = [
        dict(nsa=cache_l0_nsa, win=cache_l0_win),
        dict(dil=(cache_l1_dil_w128, cache_l1_dil_w512, cache_l1_dil_w2048)),
        dict(mla=cache_l2_mla),
        dict(nsa=cache_l3_nsa, win=cache_l3_win),
    ]
    b, t, d = x_prompt.shape
    bd, s, _ = x_sample.shape
    assert s == 1, "decode kernels handle one new token per sequence"
    past = page_table.shape[1] * cache_l0_nsa.shape[1]
    tm = _row_tile(t, 512)
    h_p = x_prompt.reshape(b * t, d)
    h_s = x_sample.reshape(bd, d)
    state = []
    for i, (p, c) in enumerate(zip(layers, caches)):
        kind = i % 3
        if kind == 0:
            h_p, nsa_p, win_p = _nsa_prompt_layer(h_p, p, b, t, tm)
            h_s, nsa_s, win_s = _nsa_sample_layer(h_s, p, c['nsa'], c['win'], page_table)
            state += [nsa_p, nsa_s, win_p, win_s]
        elif kind == 1:
            h_p, st_p = _dil_prompt_layer(h_p, p, b, t, tm)
            h_s, st_s = _dil_sample_layer(h_s, p, c['dil'], past)
            for sp, ss in zip(st_p, st_s):
                state += [sp, ss]
        else:
            h_p, mla_p = _mla_prompt_layer(h_p, p, b, t, tm)
            h_s, mla_s = _mla_sample_layer(h_s, p, c['mla'], page_table)
            state += [mla_p, mla_s]
        fin = final_norm if i == len(layers) - 1 else None
        w_up, w_down = p['w_up'].astype(BF16), p['w_down'].astype(BF16)
        h_p = _mlp(h_p, p['norm_mlp'], w_up, w_down, fin, tm)
        h_s = _mlp(h_s, p['norm_mlp'], w_up, w_down, fin, bd)
    return (h_p.reshape(b, t, d), h_s.reshape(bd, s, d), *state)
```

```python
import functools

import jax
import jax.numpy as jnp
import numpy as np
from jax import lax
from jax.experimental import pallas as pl
from jax.experimental.pallas import tpu as pltpu

F32 = jnp.float32
BF16 = jnp.bfloat16
NEG_INF = float("-inf")

HEAD_DIM = 64
ROT_DIM = HEAD_DIM // 4
ROPE_THETA = 500000.0
NORM_EPS = 1e-6
NSA_HEADS = 16
NSA_KV = 2
NSA_HPG = NSA_HEADS // NSA_KV
CMP_BLOCK = 32
CMP_STRIDE = 16
SLC_BLOCK = 64
SLC_TOPK = 16
NSA_WINDOW = 512
DIL_PAIRS = ((128, 1), (512, 4), (2048, 16))
DIL_HEADS = 8
MLA_HEADS = 16
MLA_NOPE = 64
MLA_ROPE = 32
MLA_V = 64
MLA_KV_RANK = 256
MLA_Q_RANK = 384
MLA_THETA = 10000.0
MLA_SCALE = (MLA_NOPE + MLA_ROPE) ** -0.5

LANES = 128
VMEM_LIMIT = 56 * 1024 * 1024


def _cparams(sem):
    return pltpu.CompilerParams(dimension_semantics=sem, vmem_limit_bytes=VMEM_LIMIT)


def _rms(x, gain):
    y = x * lax.rsqrt(jnp.mean(x * x, axis=-1, keepdims=True) + NORM_EPS)
    return y * gain


def _rope(x, cos, sa, sb, half):
    return x * cos + pltpu.roll(x, LANES - half, 1) * sa + pltpu.roll(x, half, 1) * sb


def _div_p2(x, n):
    assert n > 0 and n & (n - 1) == 0
    return x >> (n.bit_length() - 1)


def _mod_p2(x, n):
    assert n > 0 and n & (n - 1) == 0
    return x & (n - 1)


def _split3(x):
    hi = x.astype(BF16)
    r1 = x - hi.astype(F32)
    mid = r1.astype(BF16)
    lo = (r1 - mid.astype(F32)).astype(BF16)
    return hi, mid, lo


def _dot3(x, e):
    hi, mid, lo = _split3(x)
    d = lambda a: jnp.dot(a, e, preferred_element_type=F32)
    return d(hi) + d(mid) + d(lo)


def _dot_nt(a, b):
    return lax.dot_general(a, b, (((1,), (1,)), ((), ())), preferred_element_type=F32)


def _masked_softmax(s, mask):
    s = jnp.where(mask, s, NEG_INF)
    m = jnp.max(s, axis=-1, keepdims=True)
    m = jnp.where(m > NEG_INF, m, 0.0)
    e = jnp.exp(s - m)
    l = jnp.sum(e, axis=-1, keepdims=True)
    p = e / jnp.maximum(l, 1e-30)
    return p, m, l


def _flash_tile(s, v_ones, m_ref, acc_ref):
    m_old = m_ref[...]
    m_new = jnp.maximum(m_old, jnp.max(s, axis=-1, keepdims=True))
    m_safe = jnp.where(m_new > NEG_INF, m_new, 0.0)
    alpha = jnp.exp(m_old - m_safe)
    p = jnp.exp(s - jnp.concatenate([m_safe] * (s.shape[1] // LANES), axis=1))
    acc_ref[...] = alpha * acc_ref[...] + jnp.dot(p.astype(BF16), v_ones, preferred_element_type=F32)
    m_ref[...] = m_new


def _flash_init(m_ref, acc_ref):
    m_ref[...] = jnp.full(m_ref.shape, NEG_INF, F32)
    acc_ref[...] = jnp.zeros(acc_ref.shape, F32)


def _flash_finish(acc):
    return acc / jnp.maximum(pltpu.roll(acc, HEAD_DIM, 1), 1e-30)


def _online_init(m_ref, l_ref, acc_ref):
    m_ref[...] = jnp.full(m_ref.shape, NEG_INF, F32)
    l_ref[...] = jnp.zeros(l_ref.shape, F32)
    acc_ref[...] = jnp.zeros(acc_ref.shape, F32)


def _rope_tables(pos, rot, theta, period, offset):
    half = rot // 2
    inv = theta ** (-(jnp.arange(half, dtype=F32) * 2.0) / rot)
    ang = pos.astype(F32)[:, None] * inv[None, :]
    cos, sin = jnp.cos(ang), jnp.sin(ang)
    lane = np.arange(LANES)
    i = lane % period - offset
    first = (i >= 0) & (i < half)
    second = (i >= half) & (i < rot)
    idx = np.where(first, i, np.where(second, i - half, 0))
    cos_t = jnp.where(jnp.asarray(first | second)[None, :], cos[:, idx], 1.0)
    sa_t = jnp.where(jnp.asarray(first)[None, :], -sin[:, idx], 0.0)
    sb_t = jnp.where(jnp.asarray(second)[None, :], sin[:, idx], 0.0)
    return cos_t, sa_t, sb_t


def _row_tile(m, pref):
    t = min(m, pref)
    while m % t:
        t //= 2
    return t


NSA_QW = NSA_HEADS * HEAD_DIM
NSA_KVW = 6 * NSA_KV * HEAD_DIM
NSA_NP = NSA_QW + NSA_KVW + LANES


def _nsa_weights(w_in, w_out):
    nq, nkv = NSA_QW, NSA_KVW
    perm = []
    for j in range(NSA_HPG):
        perm += list(range(j * HEAD_DIM, (j + 1) * HEAD_DIM))
        perm += list(range((NSA_HPG + j) * HEAD_DIM, (NSA_HPG + j + 1) * HEAD_DIM))
    perm = np.asarray(perm)
    ng = 3 * NSA_HEADS
    w = jnp.concatenate([w_in[:, :nq][:, perm], w_in[:, nq:nq + nkv], w_in[:, nq + nkv:],
                         jnp.zeros((w_in.shape[0], LANES - ng), w_in.dtype)], axis=1)
    return w.astype(BF16), w_out[perm, :].astype(BF16)


def _nsa_proj_body(x_ref, g_ref, w_ref, cos_ref, sa_ref, sb_ref, q_ref, gate_ref, emit_kv):
    xn = _rms(x_ref[...], g_ref[...]).astype(BF16)
    cos, sa, sb = cos_ref[...], sa_ref[...], sb_ref[...]
    half = ROT_DIM // 2
    nqb = NSA_QW // LANES
    nkb = NSA_KVW // LANES
    for c in range(0, nqb + nkb + 1, 2):
        nb = min(2, nqb + nkb + 1 - c)
        h = jnp.dot(xn, w_ref[:, c * LANES:(c + nb) * LANES], preferred_element_type=F32)
        for u in range(nb):
            blk = c + u
            hb = h[:, u * LANES:(u + 1) * LANES]
            if blk < nqb:
                q = _rope(hb, cos, sa, sb, half) * (HEAD_DIM ** -0.5)
                q_ref[:, blk * LANES:(blk + 1) * LANES] = q.astype(BF16)
            elif blk < nqb + nkb:
                kb = blk - nqb
                if kb % 2 == 0:
                    hb = _rope(hb, cos, sa, sb, half)
                emit_kv(kb, hb)
            else:
                gate_ref[...] = 1.0 / (1.0 + jnp.exp(-hb))


def _nsa_proj_kernel(x_ref, g_ref, w_ref, cos_ref, sa_ref, sb_ref, q_ref, kvf_ref, kvb_ref, gate_ref):
    def emit_kv(kb, hb):
        kvf_ref[:, kb * LANES:(kb + 1) * LANES] = hb
        kvb_ref[:, kb * LANES:(kb + 1) * LANES] = hb.astype(BF16)
    _nsa_proj_body(x_ref, g_ref, w_ref, cos_ref, sa_ref, sb_ref, q_ref, gate_ref, emit_kv)


def _nsa_proj_prompt_kernel(x_ref, g_ref, w_ref, cos_ref, sa_ref, sb_ref, cw_ref,
                            q_ref, kvb_ref, gate_ref, nsat_ref, wint_ref, fs_ref):
    cmp_rows = {}
    n_cache = 4

    def emit_kv(kb, hb):
        kvb_ref[:, kb * LANES:(kb + 1) * LANES] = hb.astype(BF16)
        ht = jnp.transpose(hb)
        if kb < n_cache:
            nsat_ref[kb * LANES:(kb + 1) * LANES, :] = ht
        else:
            wint_ref[(kb - n_cache) * LANES:(kb - n_cache + 1) * LANES, :] = ht
        if kb < 2:
            cmp_rows[kb] = hb
    _nsa_proj_body(x_ref, g_ref, w_ref, cos_ref, sa_ref, sb_ref, q_ref, gate_ref, emit_kv)
    first, second = _chunk_sums(jnp.concatenate([cmp_rows[0], cmp_rows[1]], axis=1), cw_ref[...])
    fs_ref[:, 0:2 * LANES] = first
    fs_ref[:, 2 * LANES:4 * LANES] = second


def _nsa_proj_prompt(x, gain, w, tabs, cw, b, t, tm):
    m, d = x.shape
    nt = t // tm
    row = lambda i: (i, 0)
    tab = lambda i: (i % nt, 0)
    const = lambda i: (0, 0)
    feat = lambda i: (i // nt, 0, i % nt)
    nc = tm // CMP_STRIDE
    return pl.pallas_call(
        _nsa_proj_prompt_kernel,
        grid=(m // tm,),
        in_specs=[pl.BlockSpec((tm, d), row), pl.BlockSpec((1, d), const), pl.BlockSpec(w.shape, const),
                  pl.BlockSpec((tm, LANES), tab), pl.BlockSpec((tm, LANES), tab), pl.BlockSpec((tm, LANES), tab),
                  pl.BlockSpec(cw.shape, const)],
        out_specs=[pl.BlockSpec((tm, NSA_QW), row), pl.BlockSpec((tm, NSA_KVW), row), pl.BlockSpec((tm, LANES), row),
                   pl.BlockSpec((None, 4 * LANES, tm), feat), pl.BlockSpec((None, 2 * LANES, tm), feat),
                   pl.BlockSpec((nc, 4 * LANES), row)],
        out_shape=[jax.ShapeDtypeStruct((m, NSA_QW), BF16), jax.ShapeDtypeStruct((m, NSA_KVW), BF16),
                   jax.ShapeDtypeStruct((m, LANES), F32), jax.ShapeDtypeStruct((b, 4 * LANES, t), F32),
                   jax.ShapeDtypeStruct((b, 2 * LANES, t), F32),
                   jax.ShapeDtypeStruct((m // CMP_STRIDE, 4 * LANES), F32)],
        compiler_params=_cparams(("parallel",)),
        name="nsa_proj_prompt",
    )(x, gain.reshape(1, d), w, *tabs, cw)


def _nsa_proj(x, gain, w, tabs, tab_blocks, tm):
    m, d = x.shape
    row = lambda i: (i, 0)
    tab = lambda i: (i % tab_blocks, 0)
    const = lambda i: (0, 0)
    return pl.pallas_call(
        _nsa_proj_kernel,
        grid=(m // tm,),
        in_specs=[pl.BlockSpec((tm, d), row), pl.BlockSpec((1, d), const), pl.BlockSpec(w.shape, const),
                  pl.BlockSpec((tm, LANES), tab), pl.BlockSpec((tm, LANES), tab), pl.BlockSpec((tm, LANES), tab)],
        out_specs=[pl.BlockSpec((tm, NSA_QW), row), pl.BlockSpec((tm, NSA_KVW), row),
                   pl.BlockSpec((tm, NSA_KVW), row), pl.BlockSpec((tm, LANES), row)],
        out_shape=[jax.ShapeDtypeStruct((m, NSA_QW), BF16), jax.ShapeDtypeStruct((m, NSA_KVW), F32),
                   jax.ShapeDtypeStruct((m, NSA_KVW), BF16), jax.ShapeDtypeStruct((m, LANES), F32)],
        compiler_params=_cparams(("parallel",)),
        name="nsa_proj",
    )(x, gain.reshape(1, d), w, *tabs)


def _cmp_weights(cmp_pos, cmp_proj):
    w = jnp.concatenate([cmp_pos[0], cmp_pos[0], cmp_pos[1], cmp_pos[1]], axis=-1)
    z = jnp.zeros((HEAD_DIM, HEAD_DIM), cmp_proj.dtype)
    rows = []
    for t in range(4):
        p = cmp_proj[t // 2]
        rows.append(jnp.concatenate([p if u == t else z for u in range(4)], axis=1))
    return w.astype(F32), jnp.concatenate(rows, axis=0).astype(BF16)


def _chunk_sums(x, w):
    n = x.shape[0] // CMP_STRIDE
    x3 = x.reshape(n, CMP_STRIDE, x.shape[1])
    first = jnp.sum(x3 * w[:CMP_STRIDE][None], axis=1)
    second = jnp.sum(x3 * w[CMP_STRIDE:][None], axis=1)
    return first, second


def _cmp_prompt_kernel(fs_ref, proj_ref, o_ref):
    n = fs_ref.shape[0]
    nxt = pltpu.roll(fs_ref[:, 2 * LANES:4 * LANES], n - 1, 0)
    ridx = lax.broadcasted_iota(jnp.int32, (n, 2 * LANES), 0)
    s = jnp.where(ridx < n - 1, fs_ref[:, 0:2 * LANES] + nxt, 0.0)
    o_ref[...] = jnp.dot(s.astype(BF16), proj_ref[...], preferred_element_type=F32).astype(BF16)


def _cmp_prompt(fs, proj, b, t):
    n = t // CMP_STRIDE
    return pl.pallas_call(
        _cmp_prompt_kernel,
        grid=(b,),
        in_specs=[pl.BlockSpec((None, n, 4 * LANES), lambda i: (i, 0, 0)), pl.BlockSpec(proj.shape, lambda i: (0, 0))],
        out_specs=pl.BlockSpec((None, n, 2 * LANES), lambda i: (i, 0, 0)),
        out_shape=jax.ShapeDtypeStruct((b, n, 2 * LANES), BF16),
        compiler_params=_cparams(("parallel",)),
        name="nsa_cmp_blocks",
    )(fs.reshape(b, n, 4 * LANES), proj)


def _select_matrix(n_cmp_pad, n_sel):
    i = np.arange(n_cmp_pad)[:, None]
    j = np.arange(LANES if n_sel <= LANES else 2 * LANES)[None, :]
    s = ((i // 4 == j) | (i == 4 * j - 1)) & (j < n_sel)
    return jnp.asarray(s, BF16)


def _topk_mask(sc, n_sel, k):
    lane = lax.broadcasted_iota(jnp.int32, sc.shape, 1)
    rank = jnp.zeros(sc.shape, F32)
    for i in range(n_sel):
        col = sc[:, i:i + 1]
        beats = (col > sc) | ((col == sc) & (lane > i))
        rank = rank + jnp.where(beats, 1.0, 0.0)
    return jnp.where((rank < k) & (lane < n_sel), 1.0, 0.0)


def _rank_select_t(sc_t, n_sel, k):
    blk = lax.broadcasted_iota(jnp.int32, sc_t.shape, 0)
    rank = jnp.zeros(sc_t.shape, F32)
    for i in range(n_sel):
        row = sc_t[i:i + 1, :]
        beats = (row > sc_t) | ((row == sc_t) & (blk > i))
        rank = rank + jnp.where(beats, 1.0, 0.0)
    return jnp.where((rank < k) & (blk < n_sel), 1.0, 0.0)


def _nsa_attn2_kernel(q_ref, kv_ref, ckv_ref, gate_ref, smat_ref, o_ref,
                      qs_ref, sel_ref, bias_ref, m_ref, acc_ref, oc_ref, os_ref, ow_ref,
                      *, tq, tk, wk, n_sel):
    i = pl.program_id(1)
    q0 = i * tq
    hq = NSA_HPG * tq
    lane1 = lax.broadcasted_iota(jnp.int32, (1, LANES), 1)
    low = lane1 < HEAD_DIM

    for j in range(NSA_HPG):
        q2 = q_ref[:, j * LANES:(j + 1) * LANES]
        zero = jnp.zeros_like(q2)
        qs_ref[j * tq:(j + 1) * tq] = jnp.where(low, q2, zero)
        qs_ref[(NSA_HPG + j) * tq:(NSA_HPG + j + 1) * tq] = jnp.where(low, zero, q2)

    nc = ckv_ref.shape[0]
    ck2 = ckv_ref[:, 0:LANES]
    cv2 = ckv_ref[:, LANES:2 * LANES]
    qp_c = q0 + lax.broadcasted_iota(jnp.int32, (tq, nc), 0)
    n_c = lax.broadcasted_iota(jnp.int32, (tq, nc), 1)
    cbias = jnp.where((n_c * CMP_STRIDE + CMP_BLOCK - 1) <= qp_c, 0.0, NEG_INF)
    qp2 = q0 + lax.broadcasted_iota(jnp.int32, (tq, LANES), 0)
    blk2 = lax.broadcasted_iota(jnp.int32, (tq, LANES), 1)
    cur = qp2 >> 6
    forced = (blk2 == 0) | (blk2 == cur) | (blk2 == cur - 1)
    future = (blk2 * SLC_BLOCK) > qp2
    nsp = _round_up(n_sel, 8)
    for g in range(NSA_KV):
        rows = slice(g * hq, (g + 1) * hq)
        s = _dot_nt(qs_ref[rows], ck2).reshape(NSA_HPG, tq, nc) + cbias[None]
        m = jnp.max(s, axis=-1, keepdims=True)
        m = jnp.where(m > NEG_INF, m, 0.0)
        e = jnp.exp(s - m)
        p = (e / jnp.maximum(jnp.sum(e, axis=-1, keepdims=True), 1e-30)).reshape(hq, nc)
        oc_ref[rows] = jnp.dot(p.astype(BF16), cv2, preferred_element_type=F32)
        imp = jnp.sum(p.reshape(NSA_HPG, tq, nc), axis=0)
        sc = _dot3(imp, smat_ref[...])
        sc = jnp.where(forced, jnp.inf, jnp.where(future, NEG_INF, sc))
        sc = jnp.where(blk2 < n_sel, sc, NEG_INF)
        sel_t = _rank_select_t(jnp.transpose(sc)[0:nsp], n_sel, min(SLC_TOPK, n_sel))
        sel_t = jnp.concatenate([sel_t, jnp.zeros((LANES - nsp, tq), F32)], axis=0)
        sel_ref[g] = jnp.transpose(sel_t).astype(BF16)

    n_kt = (q0 + tq + tk - 1) // tk
    qp_k = q0 + lax.broadcasted_iota(jnp.int32, (tq, tk), 0)
    col_k = lax.broadcasted_iota(jnp.int32, (tq, tk), 1)
    eb_r = lax.broadcasted_iota(jnp.int32, (LANES, tk), 0)
    eb_c = lax.broadcasted_iota(jnp.int32, (LANES, tk), 1)
    for g in range(NSA_KV):
        rows = slice(g * hq, (g + 1) * hq)
        mine = (lane1 >= g * HEAD_DIM) & (lane1 < (g + 1) * HEAD_DIM)

        def bias_body(t, carry, g=g):
            k0 = t * tk
            expand = jnp.where(eb_r == ((k0 + eb_c) >> 6), 1.0, 0.0).astype(BF16)
            selx = jnp.dot(sel_ref[g], expand, preferred_element_type=F32)
            ok = (selx > 0.5) & ((k0 + col_k) <= qp_k)
            bias_ref[t] = jnp.where(ok, 0.0, NEG_INF)
            return carry

        lax.fori_loop(0, n_kt, bias_body, 0)
        _flash_init(m_ref, acc_ref)

        def sel_body(t, carry, rows=rows, mine=mine):
            k0 = pl.multiple_of(t * tk, tk)
            k2 = kv_ref[pl.ds(k0, tk), 2 * LANES:3 * LANES]
            v2 = kv_ref[pl.ds(k0, tk), 3 * LANES:4 * LANES]
            v_ones = jnp.where(mine, v2, jnp.ones_like(v2))
            s = _dot_nt(qs_ref[rows], k2).reshape(NSA_HPG, tq, tk) + bias_ref[t][None]
            _flash_tile(s.reshape(hq, tk), v_ones, m_ref, acc_ref)
            return carry

        lax.fori_loop(0, n_kt, sel_body, 0)
        os_ref[rows] = _flash_finish(acc_ref[...])

    ws = pl.multiple_of(jnp.maximum(q0 + tq - wk, 0), LANES)
    dist = (q0 + lax.broadcasted_iota(jnp.int32, (tq, wk), 0)) - (ws + lax.broadcasted_iota(jnp.int32, (tq, wk), 1))
    wbias = jnp.where((dist >= 0) & (dist <= NSA_WINDOW), 0.0, NEG_INF)
    for g in range(NSA_KV):
        rows = slice(g * hq, (g + 1) * hq)
        mine = (lane1 >= g * HEAD_DIM) & (lane1 < (g + 1) * HEAD_DIM)
        k2 = kv_ref[pl.ds(ws, wk), 4 * LANES:5 * LANES]
        v2 = kv_ref[pl.ds(ws, wk), 5 * LANES:6 * LANES]
        v_ones = jnp.where(mine, v2, jnp.ones_like(v2))
        s = _dot_nt(qs_ref[rows], k2).reshape(NSA_HPG, tq, wk) + wbias[None]
        m = jnp.max(s, axis=-1, keepdims=True)
        m = jnp.where(m > NEG_INF, m, 0.0)
        p = jnp.exp(s - m).reshape(hq, wk)
        ow_ref[rows] = _flash_finish(jnp.dot(p.astype(BF16), v_ones, preferred_element_type=F32))

    branch = (oc_ref, os_ref, ow_ref)
    for j in range(NSA_HPG):
        parts = []
        for g in range(NSA_KV):
            h = g * NSA_HPG + j
            r = slice(h * tq, (h + 1) * tq)
            x = jnp.zeros((tq, LANES), F32)
            for br in range(3):
                c = br * NSA_HEADS + h
                x = x + gate_ref[:, c:c + 1] * branch[br][r]
            parts.append(x)
        o_ref[:, j * LANES:(j + 1) * LANES] = jnp.where(low, parts[0], parts[1]).astype(BF16)


def _nsa_attn2(q, kvb, ckv, gates, b, t, tq=128, tk=512):
    tk = min(tk, t)
    n_sel = t // SLC_BLOCK
    nc = ckv.shape[1]
    assert nc % LANES == 0 and n_sel <= LANES and t % tk == 0
    wk = min(NSA_WINDOW + tq, t)
    smat = _select_matrix(nc, n_sel)
    hq = NSA_HPG * tq
    kern = functools.partial(_nsa_attn2_kernel, tq=tq, tk=tk, wk=wk, n_sel=n_sel)
    return pl.pallas_call(
        kern,
        grid=(b, t // tq),
        in_specs=[pl.BlockSpec((None, tq, NSA_QW), lambda bi, i: (bi, i, 0)),
                  pl.BlockSpec((None, t, NSA_KVW), lambda bi, i: (bi, 0, 0)),
                  pl.BlockSpec((None, nc, 2 * LANES), lambda bi, i: (bi, 0, 0)),
                  pl.BlockSpec((None, tq, LANES), lambda bi, i: (bi, i, 0)),
                  pl.BlockSpec(smat.shape, lambda bi, i: (0, 0))],
        out_specs=pl.BlockSpec((None, tq, NSA_QW), lambda bi, i: (bi, i, 0)),
        out_shape=jax.ShapeDtypeStruct((b, t, NSA_QW), BF16),
        scratch_shapes=[pltpu.VMEM((NSA_HEADS * tq, LANES), BF16), pltpu.VMEM((NSA_KV, tq, LANES), BF16),
                        pltpu.VMEM((t // tk, tq, tk), F32),
                        pltpu.VMEM((hq, LANES), F32), pltpu.VMEM((hq, LANES), F32),
                        pltpu.VMEM((NSA_HEADS * tq, LANES), F32), pltpu.VMEM((NSA_HEADS * tq, LANES), F32),
                        pltpu.VMEM((NSA_HEADS * tq, LANES), F32)],
        compiler_params=_cparams(("parallel", "parallel")),
        name="nsa_attn",
    )(q.reshape(b, t, NSA_QW), kvb.reshape(b, t, NSA_KVW), ckv, gates.reshape(b, t, LANES), smat)


def _out_proj_kernel(o_ref, w_ref, r_ref, y_ref):
    y_ref[...] = r_ref[...] + jnp.dot(o_ref[...], w_ref[...], preferred_element_type=F32)


def _out_proj(o, w, res, tm):
    m, k = o.shape
    d = w.shape[1]
    return pl.pallas_call(
        _out_proj_kernel,
        grid=(m // tm,),
        in_specs=[pl.BlockSpec((tm, k), lambda i: (i, 0)), pl.BlockSpec(w.shape, lambda i: (0, 0)),
                  pl.BlockSpec((tm, d), lambda i: (i, 0))],
        out_specs=pl.BlockSpec((tm, d), lambda i: (i, 0)),
        out_shape=jax.ShapeDtypeStruct((m, d), F32),
        compiler_params=_cparams(("parallel",)),
        name="out_proj",
    )(o, w, res)


def _mlp_kernel(x_ref, g_ref, wu_ref, wd_ref, fg_ref, y_ref, xn_ref, acc_ref, *, final_norm):
    j = pl.program_id(1)

    @pl.when(j == 0)
    def _():
        xn_ref[...] = _rms(x_ref[...], g_ref[...]).astype(BF16)
        acc_ref[...] = x_ref[...]

    h = jnp.maximum(jnp.dot(xn_ref[...], wu_ref[...], preferred_element_type=F32), 0.0)
    acc_ref[...] += jnp.dot((h * h).astype(BF16), wd_ref[...], preferred_element_type=F32)

    @pl.when(j == pl.num_programs(1) - 1)
    def _():
        y = acc_ref[...]
        if final_norm:
            y = _rms(y, fg_ref[...])
        y_ref[...] = y


def _mlp(x, gain, w_up, w_down, final_gain, tm, tf=512):
    m, d = x.shape
    ff = w_up.shape[1]
    final_norm = final_gain is not None
    fg = (final_gain if final_norm else gain).reshape(1, d)
    kern = functools.partial(_mlp_kernel, final_norm=final_norm)
    return pl.pallas_call(
        kern,
        grid=(m // tm, ff // tf),
        in_specs=[pl.BlockSpec((tm, d), lambda i, j: (i, 0)), pl.BlockSpec((1, d), lambda i, j: (0, 0)),
                  pl.BlockSpec((d, tf), lambda i, j: (0, j)), pl.BlockSpec((tf, d), lambda i, j: (j, 0)),
                  pl.BlockSpec((1, d), lambda i, j: (0, 0))],
        out_specs=pl.BlockSpec((tm, d), lambda i, j: (i, 0)),
        out_shape=jax.ShapeDtypeStruct((m, d), F32),
        scratch_shapes=[pltpu.VMEM((tm, d), BF16), pltpu.VMEM((tm, d), F32)],
        compiler_params=_cparams(("parallel", "arbitrary")),
        name="mlp",
    )(x, gain.reshape(1, d), w_up, w_down, fg)


DIL_GW = DIL_HEADS * HEAD_DIM
DIL_NG = len(DIL_PAIRS)


def _dil_proj_kernel(x_ref, g_ref, w_ref, cos_ref, sa_ref, sb_ref, hb_ref, st_ref, xn_ref, *, feature_major):
    j = pl.program_id(1)

    @pl.when(j == 0)
    def _():
        xn_ref[...] = _rms(x_ref[...], g_ref[...]).astype(BF16)

    cos, sa, sb = cos_ref[...], sa_ref[...], sb_ref[...]
    per = DIL_GW // LANES
    for c in range(3):
        h = jnp.dot(xn_ref[...], w_ref[:, c * DIL_GW:(c + 1) * DIL_GW], preferred_element_type=F32)
        for u in range(per):
            r = h[:, u * LANES:(u + 1) * LANES]
            if c < 2:
                r = _rope(r, cos, sa, sb, ROT_DIM // 2)
            if c > 0 and feature_major:
                st_ref[((c - 1) * per + u) * LANES:((c - 1) * per + u + 1) * LANES, :] = jnp.transpose(r)
            elif c > 0:
                st_ref[:, ((c - 1) * per + u) * LANES:((c - 1) * per + u + 1) * LANES] = r
            else:
                r = r * (HEAD_DIM ** -0.5)
            hb_ref[:, (c * per + u) * LANES:(c * per + u + 1) * LANES] = r.astype(BF16)


def _dil_proj(x, gain, w, tabs, tab_blocks, tm, prompt_batch=None):
    m, d = x.shape
    n = w.shape[1]
    gw = 3 * DIL_GW
    tab = lambda i, j: (i % tab_blocks, 0)
    if prompt_batch is None:
        st_spec = pl.BlockSpec((tm, 2 * DIL_GW), lambda i, j: (i, j))
        st_shape = jax.ShapeDtypeStruct((m, 2 * DIL_NG * DIL_GW), F32)
    else:
        nt = m // prompt_batch // tm
        st_spec = pl.BlockSpec((None, None, 2 * DIL_GW, tm), lambda i, j: (i // nt, j, 0, i % nt))
        st_shape = jax.ShapeDtypeStruct((prompt_batch, DIL_NG, 2 * DIL_GW, m // prompt_batch), F32)
    return pl.pallas_call(
        functools.partial(_dil_proj_kernel, feature_major=prompt_batch is not None),
        grid=(m // tm, n // gw),
        in_specs=[pl.BlockSpec((tm, d), lambda i, j: (i, 0)), pl.BlockSpec((1, d), lambda i, j: (0, 0)),
                  pl.BlockSpec((d, gw), lambda i, j: (0, j)),
                  pl.BlockSpec((tm, LANES), tab), pl.BlockSpec((tm, LANES), tab), pl.BlockSpec((tm, LANES), tab)],
        out_specs=[pl.BlockSpec((tm, gw), lambda i, j: (i, j)), st_spec],
        out_shape=[jax.ShapeDtypeStruct((m, n), BF16), st_shape],
        scratch_shapes=[pltpu.VMEM((tm, d), BF16)],
        compiler_params=_cparams(("parallel", "arbitrary")),
        name="dil_proj",
    )(x, gain.reshape(1, d), w, *tabs)


def _dil_attn_kernel(q_ref, k_ref, v_ref, o_ref, lse_ref, *, tq, win, ls):
    i = pl.program_id(2)
    nk = min(2 * tq, ls)
    start = pl.multiple_of(jnp.clip((i - 1) * tq, 0, ls - nk), tq)
    low = lax.broadcasted_iota(jnp.int32, (1, LANES), 1) < HEAD_DIM
    qp = i * tq + _mod_p2(lax.broadcasted_iota(jnp.int32, (2 * tq, nk), 0), tq)
    kp = start + lax.broadcasted_iota(jnp.int32, (2 * tq, nk), 1)
    dist = qp - kp
    mask = (dist >= 0) & (dist <= win)
    for jp in range(DIL_GW // LANES):
        cols = slice(jp * LANES, (jp + 1) * LANES)
        q2 = q_ref[:, cols]
        zero = jnp.zeros_like(q2)
        qs = jnp.concatenate([jnp.where(low, q2, zero), jnp.where(low, zero, q2)], axis=0)
        k2 = k_ref[pl.ds(start, nk), cols]
        v2 = v_ref[pl.ds(start, nk), cols]
        p, m, l = _masked_softmax(_dot_nt(qs, k2), mask)
        o2 = jnp.dot(p.astype(BF16), v2, preferred_element_type=F32)
        lse = m + jnp.log(jnp.maximum(l, 1e-30))
        o_ref[:, cols] = jnp.where(low, o2[:tq], o2[tq:])
        lse_ref[:, cols] = jnp.where(low, jnp.broadcast_to(lse[:tq], (tq, LANES)),
                                     jnp.broadcast_to(lse[tq:], (tq, LANES)))


def _dil_attn(hb, g, b, t, span, r, tq=128):
    ls = t // r
    n = hb.shape[1]
    per = n // DIL_GW
    hv = hb.reshape(b, ls, r * n)
    kern = functools.partial(_dil_attn_kernel, tq=tq, win=span // r, ls=ls)
    o, lse = pl.pallas_call(
        kern,
        grid=(b, r, ls // tq),
        in_specs=[pl.BlockSpec((None, tq, DIL_GW), lambda bi, rho, i: (bi, i, rho * per + 3 * g)),
                  pl.BlockSpec((None, ls, DIL_GW), lambda bi, rho, i: (bi, 0, rho * per + 3 * g + 1)),
                  pl.BlockSpec((None, ls, DIL_GW), lambda bi, rho, i: (bi, 0, rho * per + 3 * g + 2))],
        out_specs=[pl.BlockSpec((None, tq, DIL_GW), lambda bi, rho, i: (bi, i, rho)),
                   pl.BlockSpec((None, tq, DIL_GW), lambda bi, rho, i: (bi, i, rho))],
        out_shape=[jax.ShapeDtypeStruct((b, ls, r * DIL_GW), F32), jax.ShapeDtypeStruct((b, ls, r * DIL_GW), F32)],
        compiler_params=_cparams(("parallel", "parallel", "parallel")),
        name=f"dil_attn_{span}",
    )(hv, hv, hv)
    return o.reshape(b * t, DIL_GW), lse.reshape(b * t, DIL_GW)


def _dil_out_kernel(o0_ref, o1_ref, o2_ref, l0_ref, l1_ref, l2_ref, w_ref, r_ref, y_ref):
    ls = [l0_ref[...], l1_ref[...], l2_ref[...]]
    mx = jnp.maximum(jnp.maximum(ls[0], ls[1]), ls[2])
    es = [jnp.exp(l - mx) for l in ls]
    den = es[0] + es[1] + es[2]
    o = (es[0] / den) * o0_ref[...] + (es[1] / den) * o1_ref[...] + (es[2] / den) * o2_ref[...]
    y_ref[...] = r_ref[...] + jnp.dot(o.astype(BF16), w_ref[...], preferred_element_type=F32)


def _dil_out(os, lses, w, res, tm):
    m, d = res.shape
    row = lambda i: (i, 0)
    return pl.pallas_call(
        _dil_out_kernel,
        grid=(m // tm,),
        in_specs=[pl.BlockSpec((tm, DIL_GW), row)] * 6 + [pl.BlockSpec(w.shape, lambda i: (0, 0)),
                                                         pl.BlockSpec((tm, d), row)],
        out_specs=pl.BlockSpec((tm, d), row),
        out_shape=jax.ShapeDtypeStruct((m, d), F32),
        compiler_params=_cparams(("parallel",)),
        name="dil_out",
    )(*os, *lses, w, res)


MLA_QCW = MLA_HEADS * LANES
MLA_INP = 768
MLA_ST = MLA_KV_RANK + MLA_ROPE


def _mla_weights(w_in, w_uq, w_ukv):
    d = w_in.shape[0]
    n_in = MLA_Q_RANK + MLA_KV_RANK + MLA_ROPE
    win = jnp.concatenate([w_in, jnp.zeros((d, MLA_INP - n_in), w_in.dtype)], axis=1)
    uq = w_uq.reshape(MLA_Q_RANK, MLA_HEADS, MLA_NOPE + MLA_ROPE)
    uq = jnp.concatenate([uq, jnp.zeros((MLA_Q_RANK, MLA_HEADS, LANES - MLA_NOPE - MLA_ROPE), uq.dtype)], axis=2)
    ukv = w_ukv.reshape(MLA_KV_RANK, MLA_HEADS, MLA_NOPE + MLA_V)
    wk = jnp.concatenate([ukv[..., :MLA_NOPE], jnp.zeros((MLA_KV_RANK, MLA_HEADS, LANES - MLA_NOPE), ukv.dtype)],
                         axis=2)
    wv = ukv[..., MLA_NOPE:]
    return (win.astype(BF16), uq.reshape(MLA_Q_RANK, MLA_QCW).astype(BF16),
            wk.reshape(MLA_KV_RANK, MLA_QCW).astype(BF16), wv.reshape(MLA_KV_RANK, MLA_HEADS * MLA_V).astype(BF16))


def _mla_proj_kernel(x_ref, g_ref, win_ref, qg_ref, kvg_ref, wuq_ref, wk_ref, wv_ref,
                     qc_ref, qa_ref, qb_ref, kc_ref, ka_ref, kb_ref, q_ref, k_ref, v_ref, st_ref, *, feature_major):
    xn = _rms(x_ref[...], g_ref[...]).astype(BF16)
    h = jnp.dot(xn, win_ref[...], preferred_element_type=F32)
    cq = _rms(h[:, 0:MLA_Q_RANK], qg_ref[...]).astype(BF16)
    ckv = _rms(h[:, MLA_Q_RANK:MLA_Q_RANK + MLA_KV_RANK], kvg_ref[...])
    half = MLA_ROPE // 2
    kpe = _rope(h[:, MLA_Q_RANK + MLA_KV_RANK:MLA_INP], kc_ref[...], ka_ref[...], kb_ref[...], half)
    if feature_major:
        st_ref[0:MLA_KV_RANK, :] = jnp.transpose(ckv)
        st_ref[MLA_KV_RANK:MLA_ST, :] = jnp.transpose(kpe)[0:MLA_ROPE]
    else:
        st_ref[:, 0:MLA_KV_RANK] = ckv
        st_ref[:, MLA_KV_RANK:MLA_ST] = kpe[:, 0:MLA_ROPE]
    kpe_hi = pltpu.roll(kpe, MLA_NOPE, 1)
    ckv_b = ckv.astype(BF16)
    qc, qa, qb = qc_ref[...], qa_ref[...], qb_ref[...]
    for c in range(0, MLA_HEADS, 2):
        cols = slice(c * LANES, (c + 2) * LANES)
        qh = jnp.dot(cq, wuq_ref[:, cols], preferred_element_type=F32)
        kh = jnp.dot(ckv_b, wk_ref[:, cols], preferred_element_type=F32)
        for u in range(2):
            one = slice((c + u) * LANES, (c + u + 1) * LANES)
            q_ref[:, one] = _rope(qh[:, u * LANES:(u + 1) * LANES], qc, qa, qb, half).astype(BF16)
            k_ref[:, one] = (kh[:, u * LANES:(u + 1) * LANES] + kpe_hi).astype(BF16)
    v_ref[...] = jnp.dot(ckv_b, wv_ref[...], preferred_element_type=F32).astype(BF16)


def _mla_proj(x, p, ws, qtabs, ktabs, tab_blocks, tm, prompt_batch=None):
    m, d = x.shape
    win, wuq, wk, wv = ws
    row = lambda i: (i, 0)
    tab = lambda i: (i % tab_blocks, 0)
    const = lambda i: (0, 0)
    full = lambda a: pl.BlockSpec(a.shape, const)
    qg = p['q_norm'].reshape(1, -1)
    kvg = p['kv_norm'].reshape(1, -1)
    if prompt_batch is None:
        st_spec = pl.BlockSpec((tm, MLA_ST), row)
        st_shape = jax.ShapeDtypeStruct((m, MLA_ST), F32)
    else:
        nt = m // prompt_batch // tm
        st_spec = pl.BlockSpec((None, MLA_ST, tm), lambda i: (i // nt, 0, i % nt))
        st_shape = jax.ShapeDtypeStruct((prompt_batch, MLA_ST, m // prompt_batch), F32)
    return pl.pallas_call(
        functools.partial(_mla_proj_kernel, feature_major=prompt_batch is not None),
        grid=(m // tm,),
        in_specs=[pl.BlockSpec((tm, d), row), pl.BlockSpec((1, d), const), full(win), full(qg), full(kvg),
                  full(wuq), full(wk), full(wv)] + [pl.BlockSpec((tm, LANES), tab)] * 6,
        out_specs=[pl.BlockSpec((tm, MLA_QCW), row), pl.BlockSpec((tm, MLA_QCW), row),
                   pl.BlockSpec((tm, MLA_HEADS * MLA_V), row), st_spec],
        out_shape=[jax.ShapeDtypeStruct((m, MLA_QCW), BF16), jax.ShapeDtypeStruct((m, MLA_QCW), BF16),
                   jax.ShapeDtypeStruct((m, MLA_HEADS * MLA_V), BF16), st_shape],
        compiler_params=_cparams(("parallel",)),
        name="mla_proj",
    )(x, p['norm_attn'].reshape(1, d), win, qg, kvg, wuq, wk, wv, *qtabs, *ktabs)


def _mla_attn_kernel(q_ref, k_ref, v_ref, o_ref, m_ref, acc_ref, *, tq):
    i = pl.program_id(2)
    row = lax.broadcasted_iota(jnp.int32, (tq, tq), 0)
    col = lax.broadcasted_iota(jnp.int32, (tq, tq), 1)
    dbias = jnp.where(col <= row, 0.0, NEG_INF)
    low = lax.broadcasted_iota(jnp.int32, (1, LANES), 1) < MLA_V
    outs = []
    for hh in range(2):
        cols = slice(hh * LANES, (hh + 1) * LANES)
        mine = low if hh == 0 else jnp.logical_not(low)
        _flash_init(m_ref, acc_ref)

        def tile(t, bias, cols=cols, mine=mine):
            k0 = pl.multiple_of(t * tq, tq)
            s = _dot_nt(q_ref[:, cols], k_ref[pl.ds(k0, tq), cols]) * MLA_SCALE
            if bias is not None:
                s = s + bias
            v2 = v_ref[pl.ds(k0, tq), :]
            _flash_tile(s, jnp.where(mine, v2, jnp.ones_like(v2)), m_ref, acc_ref)

        def body(t, carry, tile=tile):
            tile(t, None)
            return carry

        lax.fori_loop(0, i, body, 0)
        tile(i, dbias)
        outs.append(_flash_finish(acc_ref[...]))
    o_ref[...] = jnp.where(low, outs[0], outs[1]).astype(BF16)


def _mla_attn(q, k, v, b, t, tq=512):
    tq = min(tq, t)
    kern = functools.partial(_mla_attn_kernel, tq=tq)
    return pl.pallas_call(
        kern,
        grid=(b, MLA_HEADS // 2, t // tq),
        in_specs=[pl.BlockSpec((None, tq, 2 * LANES), lambda bi, hp, i: (bi, i, hp)),
                  pl.BlockSpec((None, t, 2 * LANES), lambda bi, hp, i: (bi, 0, hp)),
                  pl.BlockSpec((None, t, LANES), lambda bi, hp, i: (bi, 0, hp))],
        out_specs=pl.BlockSpec((None, tq, LANES), lambda bi, hp, i: (bi, i, hp)),
        out_shape=jax.ShapeDtypeStruct((b, t, MLA_HEADS * MLA_V), BF16),
        scratch_shapes=[pltpu.VMEM((tq, LANES), F32), pltpu.VMEM((tq, LANES), F32)],
        compiler_params=_cparams(("parallel", "parallel", "parallel")),
        name="mla_attn",
    )(q.reshape(b, t, MLA_QCW), k.reshape(b, t, MLA_QCW), v.reshape(b, t, MLA_HEADS * MLA_V))


def _round_up(x, m):
    return (x + m - 1) // m * m


def _pool_feature_major(pool):
    n_phys, page = pool.shape[0], pool.shape[1]
    return jnp.transpose(pool, (0, 2, 3, 4, 1)).reshape(n_phys, 4 * NSA_KV * HEAD_DIM, page)


def _shift_rows_left(src_ref, dst_ref, new_row):
    f, w = src_ref.shape
    nt = w // LANES
    lane = lax.broadcasted_iota(jnp.int32, (f, LANES), 1)
    new_col = jnp.transpose(jnp.broadcast_to(new_row, (LANES, f)))
    nxt = pltpu.roll(src_ref[:, 0:LANES], LANES - 1, 1)
    for j in range(nt):
        cur = nxt
        if j + 1 < nt:
            nxt = pltpu.roll(src_ref[:, (j + 1) * LANES:(j + 2) * LANES], LANES - 1, 1)
            fill = nxt
        else:
            fill = new_col
        dst_ref[:, j * LANES:(j + 1) * LANES] = jnp.where(lane < LANES - 1, cur, fill)


def _stack_heads(q8):
    low = lax.broadcasted_iota(jnp.int32, (1, LANES), 1) < HEAD_DIM
    qf = q8.astype(F32)
    zero = jnp.zeros_like(qf)
    return jnp.concatenate([jnp.where(low, qf, zero), jnp.where(low, zero, qf)], axis=0).astype(BF16)


def _nsa_dec_cmp_kernel(pt_ref, q_ref, new_ref, pool_ref, w_ref, proj_ref, smat_ref, tri_ref,
                        oc_ref, idx_ref, cbuf, f_ref, s_ref, sem, *, n_pages, page, past, n_sel):
    b = pl.program_id(0)
    nb = pl.num_programs(0)
    slot = b % 2
    width = 2 * LANES
    new_rows = SLC_BLOCK
    n_chunks = (past + new_rows) // CMP_STRIDE
    n_cmp = n_chunks - 1
    ncp = f_ref.shape[0]

    def fetch(bb, sl):
        def body(pg, c):
            return pltpu.make_async_copy(pool_ref.at[pt_ref[bb, pg], pl.ds(0, width), :],
                                         cbuf.at[sl, pg], sem.at[sl])
        return body

    def start(bb, sl):
        mk = fetch(bb, sl)

        def body(pg, c):
            mk(pg, c).start()
            return c
        lax.fori_loop(0, n_pages, body, 0)

    @pl.when(b == 0)
    def _():
        start(0, 0)

    @pl.when(b + 1 < nb)
    def _():
        start(b + 1, 1 - slot)

    mk = fetch(b, slot)

    def wait_body(pg, c):
        mk(pg, c).wait()
        return c
    lax.fori_loop(0, n_pages, wait_body, 0)

    w = w_ref[...]
    per = page // CMP_STRIDE

    unroll = 4 if n_pages % 4 == 0 else 1

    def sum_body(c, carry):
        for u in range(unroll):
            pg = c * unroll + u
            first, second = _chunk_sums(cbuf[slot, pg].T, w)
            f_ref[pl.ds(pl.multiple_of(pg * per, per), per), :] = first
            s_ref[pl.ds(pl.multiple_of(pg * per, per), per), :] = second
        return carry
    lax.fori_loop(0, n_pages // unroll, sum_body, 0)
    tail0 = past // CMP_STRIDE
    f_ref[tail0:ncp, :] = jnp.zeros((ncp - tail0, width), F32)
    s_ref[tail0:ncp, :] = jnp.zeros((ncp - tail0, width), F32)
    ridx = lax.broadcasted_iota(jnp.int32, (new_rows, width), 0)
    first, second = _chunk_sums(jnp.where(ridx == 0, new_ref[:, 0:width], 0.0), w)
    f_ref[tail0:tail0 + new_rows // CMP_STRIDE, :] = first
    s_ref[tail0:tail0 + new_rows // CMP_STRIDE, :] = second

    nxt = pltpu.roll(s_ref[...], ncp - 1, 0)
    rid = lax.broadcasted_iota(jnp.int32, (ncp, width), 0)
    ssum = jnp.where(rid < n_cmp, f_ref[...] + nxt, 0.0)
    ckv = jnp.dot(ssum.astype(BF16), proj_ref[...], preferred_element_type=F32).astype(BF16)

    qs = _stack_heads(q_ref[...])
    s = _dot_nt(qs, ckv[:, 0:LANES])
    col = lax.broadcasted_iota(jnp.int32, (NSA_HEADS, ncp), 1)
    p, _, _ = _masked_softmax(s, (col * CMP_STRIDE + CMP_BLOCK - 1) <= past)
    oc_ref[...] = jnp.dot(p.astype(BF16), ckv[:, LANES:width], preferred_element_type=F32)

    imp = jnp.sum(p.reshape(NSA_KV, NSA_HPG, ncp), axis=1)
    row8 = lax.broadcasted_iota(jnp.int32, (8, ncp), 0)
    imp8 = jnp.where(row8 == 0, imp[0:1], jnp.where(row8 == 1, imp[1:2], 0.0))
    sc = _dot3(imp8, smat_ref[...])
    nl = sc.shape[1]
    blk = lax.broadcasted_iota(jnp.int32, (8, nl), 1)
    cur = past // SLC_BLOCK
    forced = (blk == 0) | (blk == cur) | (blk == cur - 1)
    future = (blk * SLC_BLOCK) > past
    sc = jnp.where(forced, jnp.inf, jnp.where(future, NEG_INF, sc))
    sc = jnp.where(blk < n_sel, sc, NEG_INF)
    k = min(SLC_TOPK, n_sel)
    sel = _topk_mask(sc, n_sel, k)
    cnt = jnp.dot(sel.astype(BF16), tri_ref[...], preferred_element_type=F32)
    slot_id = lax.broadcasted_iota(jnp.int32, (SLC_TOPK, nl), 0).astype(F32)
    blk_f = lax.broadcasted_iota(jnp.int32, (SLC_TOPK, nl), 1).astype(F32)
    for g in range(NSA_KV):
        hit = (sel[g:g + 1] > 0.5) & (cnt[g:g + 1] == slot_id)
        ids = jnp.sum(jnp.where(hit, blk_f, 0.0), axis=-1, keepdims=True)
        idx_ref[g * SLC_TOPK:(g + 1) * SLC_TOPK, :] = jnp.broadcast_to(ids, (SLC_TOPK, LANES)).astype(jnp.int32)


def _nsa_dec_cmp(q8, kvf, pool, page_table, cw, cproj):
    bd, n_pages = page_table.shape
    n_phys, page = pool.shape[0], pool.shape[1]
    past = n_pages * page
    n_chunks = (past + SLC_BLOCK) // CMP_STRIDE
    ncp = _round_up(n_chunks, LANES)
    n_sel = past // SLC_BLOCK + 1
    assert n_sel <= 2 * LANES and SLC_TOPK <= n_sel
    smat = _select_matrix(ncp, n_sel)
    nl = smat.shape[1]
    tri = jnp.asarray(np.triu(np.ones((nl, nl)), 1), BF16)
    pool3 = _pool_feature_major(pool)
    kern = functools.partial(_nsa_dec_cmp_kernel, n_pages=n_pages, page=page, past=past, n_sel=n_sel)
    const = lambda i, pt: (0, 0)
    grid_spec = pltpu.PrefetchScalarGridSpec(
        num_scalar_prefetch=1,
        grid=(bd,),
        in_specs=[pl.BlockSpec((None, NSA_HPG, LANES), lambda i, pt: (i, 0, 0)),
                  pl.BlockSpec((None, 1, NSA_KVW), lambda i, pt: (i, 0, 0)),
                  pl.BlockSpec(memory_space=pl.ANY),
                  pl.BlockSpec(cw.shape, const), pl.BlockSpec(cproj.shape, const),
                  pl.BlockSpec(smat.shape, const), pl.BlockSpec(tri.shape, const)],
        out_specs=[pl.BlockSpec((None, NSA_HEADS, LANES), lambda i, pt: (i, 0, 0)),
                   pl.BlockSpec((None, NSA_KV * SLC_TOPK, LANES), lambda i, pt: (i, 0, 0))],
        scratch_shapes=[pltpu.VMEM((2, n_pages, 2 * LANES, page), F32),
                        pltpu.VMEM((ncp, 2 * LANES), F32), pltpu.VMEM((ncp, 2 * LANES), F32),
                        pltpu.SemaphoreType.DMA((2,))])
    return pl.pallas_call(
        kern,
        grid_spec=grid_spec,
        out_shape=[jax.ShapeDtypeStruct((bd, NSA_HEADS, LANES), F32),
                   jax.ShapeDtypeStruct((bd, NSA_KV * SLC_TOPK, LANES), jnp.int32)],
        compiler_params=_cparams(("arbitrary",)),
        name="nsa_dec_cmp",
    )(page_table, q8, kvf.reshape(bd, 1, NSA_KVW), pool3, cw, cproj, smat, tri)


def _nsa_dec_sel_kernel(pt_ref, ids_ref, q_ref, new_ref, gate_ref, oc_ref, idl_ref, win_ref, pool_ref,
                        o_ref, wout_ref, sbuf, sem, *, past, wb):
    b = pl.program_id(0)
    nb = pl.num_programs(0)
    slot = b % 2
    width = 2 * LANES
    n_slots = NSA_KV * SLC_TOPK
    n_past = past // SLC_BLOCK
    page = pool_ref.shape[2]
    sub = page // SLC_BLOCK

    def copy(bb, sl, s):
        ip = jnp.minimum(ids_ref[bb, s], n_past - 1)
        pg = pt_ref[bb, ip // sub]
        return pltpu.make_async_copy(pool_ref.at[pg, pl.ds(width, width), :], sbuf.at[sl, s], sem.at[sl])

    def start(bb, sl):
        def body(s, c):
            copy(bb, sl, s).start()
            return c
        lax.fori_loop(0, n_slots, body, 0)

    @pl.when(b == 0)
    def _():
        start(0, 0)

    @pl.when(b + 1 < nb)
    def _():
        start(b + 1, 1 - slot)

    def wait_body(s, c):
        copy(b, slot, s).wait()
        return c
    lax.fori_loop(0, n_slots, wait_body, 0)

    qs = _stack_heads(q_ref[...])
    new = new_ref[...]
    nk = SLC_TOPK * page
    low = lax.broadcasted_iota(jnp.int32, (1, LANES), 1) < HEAD_DIM

    new_sb = new[:, width:2 * width].astype(BF16).astype(F32)
    ecol = jnp.where(lax.broadcasted_iota(jnp.int32, (LANES, nk), 0)
                     == _div_p2(lax.broadcasted_iota(jnp.int32, (LANES, nk), 1), page), 1.0, 0.0).astype(BF16)
    blk_in_page = _div_p2(_mod_p2(lax.broadcasted_iota(jnp.int32, (8, nk), 1), page), SLC_BLOCK)
    o_sel = []
    for g in range(NSA_KV):
        qg = qs[g * NSA_HPG:(g + 1) * NSA_HPG]
        idl = idl_ref[:, g * LANES:(g + 1) * LANES]
        blk_l = jnp.dot(idl.astype(BF16), ecol, preferred_element_type=F32).astype(jnp.int32)
        from_pool = blk_l < n_past
        valid = from_pool & (_mod_p2(blk_l, sub) == blk_in_page)
        has_new = jnp.max(jnp.where(from_pool, 0.0, 1.0), axis=-1, keepdims=True) > 0.5
        kt = jnp.concatenate([sbuf[slot, g * SLC_TOPK + k, 0:LANES, :] for k in range(SLC_TOPK)],
                             axis=1).astype(BF16)
        vt = jnp.concatenate([sbuf[slot, g * SLC_TOPK + k, LANES:width, :] for k in range(SLC_TOPK)],
                             axis=1).astype(BF16)
        s = jnp.where(valid, jnp.dot(qg, kt, preferred_element_type=F32), NEG_INF)
        s_n = jnp.sum(qg.astype(F32) * new_sb[:, 0:LANES], axis=-1, keepdims=True)
        s_n = jnp.where(has_new, s_n, NEG_INF)
        m = jnp.maximum(jnp.max(s, axis=-1, keepdims=True), s_n)
        m = jnp.where(m > NEG_INF, m, 0.0)
        e = jnp.exp(s - m)
        e_n = jnp.exp(s_n - m)
        l = jnp.maximum(jnp.sum(e, axis=-1, keepdims=True) + e_n, 1e-30)
        o_sel.append(_dot_nt((e / l).astype(BF16), vt)
                     + (e_n / l).astype(BF16).astype(F32) * new_sb[:, LANES:width])
    os_ = jnp.concatenate(o_sel, axis=0)

    win = win_ref[...]
    winb = win.astype(BF16)
    new_win = new[:, 2 * width:3 * width]
    new_wb = new_win.astype(BF16).astype(F32)
    s_w = jnp.dot(qs, winb[0:LANES], preferred_element_type=F32)
    s_n = jnp.sum(qs.astype(F32) * new_wb[:, 0:LANES], axis=-1, keepdims=True)
    kidx = lax.broadcasted_iota(jnp.int32, (NSA_HEADS, wb), 1)
    dist = wb - kidx
    s_w = jnp.where(dist <= NSA_WINDOW, s_w, NEG_INF)
    m = jnp.maximum(jnp.max(s_w, axis=-1, keepdims=True), s_n)
    e_w = jnp.exp(s_w - m)
    e_n = jnp.exp(s_n - m)
    l = jnp.maximum(jnp.sum(e_w, axis=-1, keepdims=True) + e_n, 1e-30)
    ow = (_dot_nt((e_w / l).astype(BF16), winb[LANES:width])
          + (e_n / l).astype(BF16).astype(F32) * new_wb[:, LANES:width])

    gl = lax.broadcasted_iota(jnp.int32, (NSA_HEADS, LANES), 1)
    gh = lax.broadcasted_iota(jnp.int32, (NSA_HEADS, LANES), 0)
    grow = jnp.broadcast_to(gate_ref[...], (NSA_HEADS, LANES))
    gcol = lambda br: jnp.sum(jnp.where(gl == br * NSA_HEADS + gh, grow, 0.0), axis=-1, keepdims=True)
    x = gcol(0) * oc_ref[...] + gcol(1) * os_ + gcol(2) * ow
    o_ref[...] = jnp.where(low, x[0:NSA_HPG], x[NSA_HPG:]).astype(BF16)

    _shift_rows_left(win_ref, wout_ref, new_win)


def _nsa_dec_sel(q8, kvf, gates, oc, ids, pool, win_buf, page_table):
    bd, n_pages = page_table.shape
    n_phys, page = pool.shape[0], pool.shape[1]
    past = n_pages * page
    wb = win_buf.shape[1]
    ids2 = ids[:, :, 0]
    idl = ids2.astype(F32).reshape(bd, NSA_KV, SLC_TOPK)
    idl = jnp.pad(idl, ((0, 0), (0, 0), (0, LANES - SLC_TOPK))).reshape(bd, 1, NSA_KV * LANES)
    idl = jnp.broadcast_to(idl, (bd, 8, NSA_KV * LANES))
    assert wb % LANES == 0 and page % SLC_BLOCK == 0
    pool3 = _pool_feature_major(pool)
    win3 = jnp.transpose(win_buf, (0, 2, 3, 4, 1)).reshape(bd, 2 * LANES, wb)
    kern = functools.partial(_nsa_dec_sel_kernel, past=past, wb=wb)
    row3 = lambda i, pt, sid: (i, 0, 0)
    grid_spec = pltpu.PrefetchScalarGridSpec(
        num_scalar_prefetch=2,
        grid=(bd,),
        in_specs=[pl.BlockSpec((None, NSA_HPG, LANES), row3), pl.BlockSpec((None, 1, NSA_KVW), row3),
                  pl.BlockSpec((None, 1, LANES), row3), pl.BlockSpec((None, NSA_HEADS, LANES), row3),
                  pl.BlockSpec((None, 8, NSA_KV * LANES), row3), pl.BlockSpec((None, 2 * LANES, wb), row3),
                  pl.BlockSpec(memory_space=pl.ANY)],
        out_specs=[pl.BlockSpec((None, NSA_HPG, LANES), row3), pl.BlockSpec((None, 2 * LANES, wb), row3)],
        scratch_shapes=[pltpu.VMEM((2, NSA_KV * SLC_TOPK, 2 * LANES, page), F32),
                        pltpu.SemaphoreType.DMA((2,))])
    o, wout = pl.pallas_call(
        kern,
        grid_spec=grid_spec,
        out_shape=[jax.ShapeDtypeStruct((bd, NSA_HPG, LANES), BF16),
                   jax.ShapeDtypeStruct((bd, 2 * LANES, wb), F32)],
        compiler_params=_cparams(("arbitrary",)),
        name="nsa_dec_sel",
    )(page_table, ids2, q8, kvf.reshape(bd, 1, NSA_KVW), gates.reshape(bd, 1, LANES), oc, idl, win3, pool3)
    wout = jnp.transpose(wout.reshape(bd, 2, NSA_KV, HEAD_DIM, wb), (0, 4, 1, 2, 3))
    return o, wout


def _dil_dec_kernel(hq_ref, st_ref, c0_ref, c1_ref, c2_ref, o_ref, s0_ref, s1_ref, s2_ref):
    c = pl.program_id(1)
    caches = (c0_ref, c1_ref, c2_ref)
    outs = (s0_ref, s1_ref, s2_ref)
    tiles = DIL_GW // LANES
    low = lax.broadcasted_iota(jnp.int32, (1, LANES), 1) < HEAD_DIM
    row = lax.broadcasted_iota(jnp.int32, (8, LANES), 0)
    lane8 = lax.broadcasted_iota(jnp.int32, (8, LANES), 1)
    mine = ((row == 0) & (lane8 < HEAD_DIM)) | ((row == 1) & (lane8 >= HEAD_DIM))
    o_g, lse_g = [], []
    for g, (span, r) in enumerate(DIL_PAIRS):
        wb = caches[g].shape[2]
        q = hq_ref[3 * g * tiles + c].astype(F32)
        kn = st_ref[2 * g * tiles + c]
        vn = st_ref[(2 * g + 1) * tiles + c]
        q2 = jnp.where(mine, jnp.broadcast_to(q, (8, LANES)), 0.0)
        knb = kn.astype(BF16).astype(F32)
        vnb = vn.astype(BF16).astype(F32)
        kt = caches[g][0].astype(BF16)
        vt = caches[g][1].astype(BF16)
        s = jnp.dot(q2.astype(BF16), kt, preferred_element_type=F32)
        t = lax.broadcasted_iota(jnp.int32, (8, wb), 1)
        s = jnp.where(_mod_p2(t, r) == 0, s, NEG_INF)
        s_n = jnp.sum(q2 * knb, axis=-1, keepdims=True)
        m = jnp.maximum(jnp.max(s, axis=-1, keepdims=True), s_n)
        e = jnp.exp(s - m)
        e_n = jnp.exp(s_n - m)
        l = jnp.maximum(jnp.sum(e, axis=-1, keepdims=True) + e_n, 1e-30)
        o_g.append(_dot_nt((e / l).astype(BF16), vt) + (e_n / l).astype(BF16).astype(F32) * vnb)
        lse_g.append(m + jnp.log(l))
        _shift_rows_left(caches[g].at[0], outs[g].at[0], kn)
        _shift_rows_left(caches[g].at[1], outs[g].at[1], vn)
    mx = jnp.maximum(jnp.maximum(lse_g[0], lse_g[1]), lse_g[2])
    es = [jnp.exp(x - mx) for x in lse_g]
    den = es[0] + es[1] + es[2]
    o = (es[0] / den) * o_g[0] + (es[1] / den) * o_g[1] + (es[2] / den) * o_g[2]
    o_ref[...] = jnp.where(low, o[0:1], o[1:2]).astype(BF16)


def _dil_dec(hb, st, bufs):
    bd = hb.shape[0]
    tiles = DIL_GW // LANES
    views, specs, shapes = [], [], []
    for (span, r), buf in zip(DIL_PAIRS, bufs):
        wb = buf.shape[1]
        assert wb == span and wb % r == 0 and wb % LANES == 0
        views.append(jnp.transpose(buf, (0, 2, 3, 4, 1)).reshape(bd, 2, tiles, LANES, wb))
        specs.append(pl.BlockSpec((None, 2, None, LANES, wb), lambda i, c: (i, 0, c, 0, 0)))
        shapes.append(jax.ShapeDtypeStruct((bd, 2, tiles, LANES, wb), F32))
    nq = hb.shape[1] // LANES
    ns = st.shape[1] // LANES
    res = pl.pallas_call(
        _dil_dec_kernel,
        grid=(bd, tiles),
        in_specs=[pl.BlockSpec((None, nq, 1, LANES), lambda i, c: (i, 0, 0, 0)),
                  pl.BlockSpec((None, ns, 1, LANES), lambda i, c: (i, 0, 0, 0))] + specs,
        out_specs=[pl.BlockSpec((None, None, 1, LANES), lambda i, c: (i, c, 0, 0))] + specs,
        out_shape=[jax.ShapeDtypeStruct((bd, tiles, 1, LANES), BF16)] + shapes,
        compiler_params=_cparams(("parallel", "parallel")),
        name="dil_dec",
    )(hb.reshape(bd, nq, 1, LANES), st.reshape(bd, ns, 1, LANES), *views)
    states = [jnp.transpose(s.reshape(bd, 2, DIL_HEADS, HEAD_DIM, s.shape[-1]), (0, 4, 1, 2, 3)) for s in res[1:]]
    return res[0].reshape(bd, DIL_GW), states


def _mla_qlat_kernel(q_ref, wkt_ref, ql_ref, qp_ref):
    for h in range(MLA_HEADS):
        qh = q_ref[:, h * LANES:(h + 1) * LANES]
        ql_ref[h] = jnp.dot(qh, wkt_ref[h], preferred_element_type=F32).astype(BF16)
        lane = lax.broadcasted_iota(jnp.int32, qh.shape, 1)
        qpe = jnp.where((lane >= MLA_NOPE) & (lane < MLA_NOPE + MLA_ROPE), qh.astype(F32), 0.0)
        qp_ref[h] = pltpu.roll(qpe, LANES - MLA_NOPE - MLA_ROPE, 1).astype(BF16)


def _mla_qlat(q, wkt):
    bd = q.shape[0]
    return pl.pallas_call(
        _mla_qlat_kernel,
        out_shape=[jax.ShapeDtypeStruct((MLA_HEADS, bd, MLA_KV_RANK), BF16),
                   jax.ShapeDtypeStruct((MLA_HEADS, bd, LANES), BF16)],
        compiler_params=pltpu.CompilerParams(vmem_limit_bytes=VMEM_LIMIT),
        name="mla_qlat",
    )(q, wkt)


def _mla_dec_kernel(pt_ref, ql_ref, qp_ref, new_ref, pool_ref, o_ref, buf, m_ref, l_ref, acc_ref, sem,
                    *, n_pages, page, past, chunk):
    b = pl.program_id(0)
    nb = pl.num_programs(0)
    slot = b % 2

    def fetch(bb, sl):
        def mk(pg):
            return pltpu.make_async_copy(pool_ref.at[pt_ref[bb, pg]], buf.at[sl, pg], sem.at[sl])
        return mk

    def start(bb, sl):
        mk = fetch(bb, sl)

        def body(pg, c):
            mk(pg).start()
            return c
        lax.fori_loop(0, n_pages, body, 0)

    @pl.when(b == 0)
    def _():
        start(0, 0)

    @pl.when(b + 1 < nb)
    def _():
        start(b + 1, 1 - slot)

    mk = fetch(b, slot)

    def wait_body(pg, c):
        mk(pg).wait()
        return c
    lax.fori_loop(0, n_pages, wait_body, 0)

    ql = ql_ref[...]
    qp = qp_ref[...]
    pe0 = MLA_ST - LANES
    _online_init(m_ref, l_ref, acc_ref)
    ppc = chunk // page

    def body(c, carry):
        x = jnp.concatenate([buf[slot, c * ppc + u] for u in range(ppc)], axis=1).astype(BF16)
        lat = x[0:MLA_KV_RANK]
        s = (jnp.dot(ql, lat, preferred_element_type=F32)
             + jnp.dot(qp, x[pe0:MLA_ST], preferred_element_type=F32)) * MLA_SCALE
        m_old = m_ref[...]
        m_new = jnp.maximum(m_old, jnp.max(s, axis=-1, keepdims=True))
        alpha = jnp.exp(m_old - m_new)
        p = jnp.exp(s - m_new)
        l_ref[...] = alpha * l_ref[...] + jnp.sum(p, axis=-1, keepdims=True)
        acc_ref[...] = alpha * acc_ref[...] + _dot_nt(p.astype(BF16), lat)
        m_ref[...] = m_new
        return carry
    lax.fori_loop(0, past // chunk, body, 0)

    newb = new_ref[...].astype(BF16).astype(F32)
    s_n = (jnp.sum(ql.astype(F32) * newb[:, 0:MLA_KV_RANK], axis=-1, keepdims=True)
           + jnp.sum(qp.astype(F32)[:, LANES - MLA_ROPE:] * newb[:, MLA_KV_RANK:MLA_ST], axis=-1, keepdims=True)
           ) * MLA_SCALE
    m_old = m_ref[...]
    m_new = jnp.maximum(m_old, s_n)
    alpha = jnp.exp(m_old - m_new)
    p_n = jnp.exp(s_n - m_new)
    l = jnp.maximum(alpha * l_ref[...] + p_n, 1e-30)
    acc = alpha * acc_ref[...] + p_n.astype(BF16).astype(F32) * newb[:, 0:MLA_KV_RANK]
    o_ref[...] = acc / l


def _mla_dec(ql, qp, st, pool, page_table, chunk=1024):
    bd, n_pages = page_table.shape
    page = pool.shape[1]
    past = n_pages * page
    chunk = min(chunk, past)
    kern = functools.partial(_mla_dec_kernel, n_pages=n_pages, page=page, past=past, chunk=chunk)
    row3 = lambda i, pt: (i, 0, 0)
    grid_spec = pltpu.PrefetchScalarGridSpec(
        num_scalar_prefetch=1,
        grid=(bd,),
        in_specs=[pl.BlockSpec((None, MLA_HEADS, MLA_KV_RANK), row3), pl.BlockSpec((None, MLA_HEADS, LANES), row3),
                  pl.BlockSpec((None, 1, MLA_ST), row3), pl.BlockSpec(memory_space=pl.ANY)],
        out_specs=pl.BlockSpec((None, MLA_HEADS, MLA_KV_RANK), row3),
        scratch_shapes=[pltpu.VMEM((2, n_pages, MLA_ST, page), F32), pltpu.VMEM((MLA_HEADS, 1), F32),
                        pltpu.VMEM((MLA_HEADS, 1), F32), pltpu.VMEM((MLA_HEADS, MLA_KV_RANK), F32),
                        pltpu.SemaphoreType.DMA((2,))])
    return pl.pallas_call(
        kern,
        grid_spec=grid_spec,
        out_shape=jax.ShapeDtypeStruct((bd, MLA_HEADS, MLA_KV_RANK), F32),
        compiler_params=_cparams(("arbitrary",)),
        name="mla_dec",
    )(page_table, ql, qp, st.reshape(bd, 1, MLA_ST), jnp.transpose(pool, (0, 2, 1)))


def _mla_dec_out_kernel(ol_ref, wvp_ref, w_ref, r_ref, y_ref, o_scr):
    for j in range(MLA_HEADS // 2):
        acc = jnp.dot(ol_ref[2 * j].astype(BF16), wvp_ref[2 * j], preferred_element_type=F32)
        acc = acc + jnp.dot(ol_ref[2 * j + 1].astype(BF16), wvp_ref[2 * j + 1], preferred_element_type=F32)
        o_scr[:, j * LANES:(j + 1) * LANES] = acc.astype(BF16)
    y_ref[...] = r_ref[...] + jnp.dot(o_scr[...], w_ref[...], preferred_element_type=F32)


def _mla_dec_out(ol, wvp, w_out, res):
    bd, d = res.shape
    return pl.pallas_call(
        _mla_dec_out_kernel,
        out_shape=jax.ShapeDtypeStruct((bd, d), F32),
        scratch_shapes=[pltpu.VMEM((bd, MLA_HEADS * MLA_V), BF16)],
        compiler_params=pltpu.CompilerParams(vmem_limit_bytes=VMEM_LIMIT),
        name="mla_dec_out",
    )(ol, wvp, w_out, res)


def _dil_sample_layer(h, p, bufs, past):
    bd = h.shape[0]
    tabs = _rope_tables(jnp.full((bd,), past), ROT_DIM, ROPE_THETA, HEAD_DIM, 0)
    hb, st = _dil_proj(h, p['norm_attn'], p['w_in'].astype(BF16), tabs, 1, bd)
    o, states = _dil_dec(hb, st, bufs)
    y = _out_proj(o, p['w_out'].astype(BF16), h, bd)
    return y, states


def _mla_sample_layer(h, p, pool, page_table):
    bd = h.shape[0]
    past = page_table.shape[1] * pool.shape[1]
    ws = _mla_weights(p['w_in'], p['w_uq'], p['w_ukv'])
    pos = jnp.full((bd,), past)
    qtabs = _rope_tables(pos, MLA_ROPE, MLA_THETA, LANES, MLA_NOPE)
    ktabs = _rope_tables(pos, MLA_ROPE, MLA_THETA, LANES, 0)
    q, _, _, st = _mla_proj(h, p, ws, qtabs, ktabs, 1, bd)
    ukv = p['w_ukv'].reshape(MLA_KV_RANK, MLA_HEADS, MLA_NOPE + MLA_V)
    wkt = jnp.transpose(ukv[..., :MLA_NOPE], (1, 2, 0))
    wkt = jnp.concatenate([wkt, jnp.zeros((MLA_HEADS, LANES - MLA_NOPE, MLA_KV_RANK), wkt.dtype)], axis=1)
    ql, qp = _mla_qlat(q, wkt.astype(BF16))
    ol = _mla_dec(jnp.transpose(ql, (1, 0, 2)), jnp.transpose(qp, (1, 0, 2)), st, pool, page_table)
    wv = jnp.transpose(ukv[..., MLA_NOPE:], (1, 0, 2))
    z = jnp.zeros_like(wv)
    even = (np.arange(MLA_HEADS) % 2 == 0)[:, None, None]
    wvp = jnp.concatenate([jnp.where(even, wv, z), jnp.where(even, z, wv)], axis=2)
    y = _mla_dec_out(jnp.transpose(ol, (1, 0, 2)), wvp.astype(BF16), p['w_out'].astype(BF16), h)
    return y, st.reshape(bd, 1, MLA_ST)


def _nsa_sample_layer(h, p, pool, win_buf, page_table):
    bd = h.shape[0]
    past = page_table.shape[1] * pool.shape[1]
    w_in, w_out = _nsa_weights(p['w_in'], p['w_out'])
    tabs = _rope_tables(jnp.full((bd,), past), ROT_DIM, ROPE_THETA, HEAD_DIM, 0)
    q, kvf, _, gates = _nsa_proj(h, p['norm_attn'], w_in, tabs, 1, bd)
    q8 = q.reshape(bd, NSA_HPG, LANES)
    cw, cproj = _cmp_weights(p['cmp_pos'], p['cmp_proj'])
    oc, ids = _nsa_dec_cmp(q8, kvf, pool, page_table, cw, cproj)
    o, wout = _nsa_dec_sel(q8, kvf, gates, oc, ids, pool, win_buf, page_table)
    y = _out_proj(o.reshape(bd, NSA_QW), w_out, h, bd)
    kv5 = kvf.reshape(bd, 1, 6, NSA_KV, HEAD_DIM)
    return y, kv5[:, :, 0:4], wout.reshape(win_buf.shape)


def _dil_prompt_layer(h, p, b, t, tm):
    w_in = p['w_in'].astype(BF16)
    tabs = _rope_tables(jnp.arange(t), ROT_DIM, ROPE_THETA, HEAD_DIM, 0)
    hb, st = _dil_proj(h, p['norm_attn'], w_in, tabs, t // tm, tm, prompt_batch=b)
    os, lses, states = [], [], []
    for g, (span, r) in enumerate(DIL_PAIRS):
        o, lse = _dil_attn(hb, g, b, t, span, r)
        os.append(o)
        lses.append(lse)
        keep = min(span, t)
        s = st[:, g, :, t - keep:].reshape(b, 2, DIL_HEADS, HEAD_DIM, keep)
        states.append(jnp.transpose(s, (0, 4, 1, 2, 3)))
    y = _dil_out(os, lses, p['w_out'].astype(BF16), h, tm)
    return y, states


def _mla_prompt_layer(h, p, b, t, tm):
    ws = _mla_weights(p['w_in'], p['w_uq'], p['w_ukv'])
    pos = jnp.arange(t)
    qtabs = _rope_tables(pos, MLA_ROPE, MLA_THETA, LANES, MLA_NOPE)
    ktabs = _rope_tables(pos, MLA_ROPE, MLA_THETA, LANES, 0)
    q, k, v, st = _mla_proj(h, p, ws, qtabs, ktabs, t // tm, tm, prompt_batch=b)
    o = _mla_attn(q, k, v, b, t)
    y = _out_proj(o.reshape(b * t, MLA_HEADS * MLA_V), p['w_out'].astype(BF16), h, tm)
    return y, jnp.transpose(st, (0, 2, 1))


def _nsa_prompt_layer(h, p, b, t, tm):
    w_in, w_out = _nsa_weights(p['w_in'], p['w_out'])
    tabs = _rope_tables(jnp.arange(t), ROT_DIM, ROPE_THETA, HEAD_DIM, 0)
    cw, cproj = _cmp_weights(p['cmp_pos'], p['cmp_proj'])
    q, kvb, gates, nsa_t, win_t, fs = _nsa_proj_prompt(h, p['norm_attn'], w_in, tabs, cw, b, t, tm)
    ckv = _cmp_prompt(fs, cproj, b, t)
    o = _nsa_attn2(q, kvb, ckv, gates, b, t)
    y = _out_proj(o.reshape(b * t, NSA_QW), w_out, h, tm)
    nsa_p = jnp.transpose(nsa_t.reshape(b, 4, NSA_KV, HEAD_DIM, t), (0, 4, 1, 2, 3))
    wk = min(NSA_WINDOW, t)
    win_p = jnp.transpose(win_t[:, :, t - wk:].reshape(b, 2, NSA_KV, HEAD_DIM, wk), (0, 4, 1, 2, 3))
    return y, nsa_p, win_p


def kernel(x_prompt, x_sample, cache_l0_nsa, cache_l0_win, cache_l1_dil_w128, cache_l1_dil_w512, cache_l1_dil_w2048, cache_l2_mla, cache_l3_nsa, cache_l3_win, page_table, l0_norm_attn, l0_w_in, l0_cmp_pos, l0_cmp_proj, l0_w_out, l0_norm_mlp, l0_w_up, l0_w_down, l1_norm_attn, l1_w_in, l1_w_out, l1_norm_mlp, l1_w_up, l1_w_down, l2_norm_attn, l2_w_in, l2_q_norm, l2_w_uq, l2_kv_norm, l2_w_ukv, l2_w_out, l2_norm_mlp, l2_w_up, l2_w_down, l3_norm_attn, l3_w_in, l3_cmp_pos, l3_cmp_proj, l3_w_out, l3_norm_mlp, l3_w_up, l3_w_down, final_norm):
    layers = [
        dict(norm_attn=l0_norm_attn, w_in=l0_w_in, cmp_pos=l0_cmp_pos, cmp_proj=l0_cmp_proj, w_out=l0_w_out,
             norm_mlp=l0_norm_mlp, w_up=l0_w_up, w_down=l0_w_down),
        dict(norm_attn=l1_norm_attn, w_in=l1_w_in, w_out=l1_w_out,
             norm_mlp=l1_norm_mlp, w_up=l1_w_up, w_down=l1_w_down),
        dict(norm_attn=l2_norm_attn, w_in=l2_w_in, q_norm=l2_q_norm, w_uq=l2_w_uq, kv_norm=l2_kv_norm,
             w_ukv=l2_w_ukv, w_out=l2_w_out, norm_mlp=l2_norm_mlp, w_up=l2_w_up, w_down=l2_w_down),
        dict(norm_attn=l3_norm_attn, w_in=l3_w_in, cmp_pos=l3_cmp_pos, cmp_proj=l3_cmp_proj, w_out=l3_w_out,
             norm_mlp=l3_norm_mlp, w_up=l3_w_up, w_down=l3_w_down),
    ]
    caches = [
        dict(nsa=cache_l0_nsa, win=cache_l0_win),
        dict(dil=(cache_l1_dil_w128, cache_l1_dil_w512, cache_l1_dil_w2048)),
        dict(mla=cache_l2_mla),
        dict(nsa=cache_l3_nsa, win=cache_l3_win),
    ]
    b, t, d = x_prompt.shape
    bd, s, _ = x_sample.shape
    assert s == 1, "decode kernels handle one new token per sequence"
    past = page_table.shape[1] * cache_l0_nsa.shape[1]
    tm = _row_tile(t, 512)
    h_p = x_prompt.reshape(b * t, d)
    h_s = x_sample.reshape(bd, d)
    state = []
    for i, (p, c) in enumerate(zip(layers, caches)):
        kind = i % 3
        if kind == 0:
            h_p, nsa_p, win_p = _nsa_prompt_layer(h_p, p, b, t, tm)
            h_s, nsa_s, win_s = _nsa_sample_layer(h_s, p, c['nsa'], c['win'], page_table)
            state += [nsa_p, nsa_s, win_p, win_s]
        elif kind == 1:
            h_p, st_p = _dil_prompt_layer(h_p, p, b, t, tm)
            h_s, st_s = _dil_sample_layer(h_s, p, c['dil'], past)
            for sp, ss in zip(st_p, st_s):
                state += [sp, ss]
        else:
            h_p, mla_p = _mla_prompt_layer(h_p, p, b, t, tm)
            h_s, mla_s = _mla_sample_layer(h_s, p, c['mla'], page_table)
            state += [mla_p, mla_s]
        fin = final_norm if i == len(layers) - 1 else None
        w_up, w_down = p['w_up'].astype(BF16), p['w_down'].astype(BF16)
        h_p = _mlp(h_p, p['norm_mlp'], w_up, w_down, fin, tm)
        h_s = _mlp(h_s, p['norm_mlp'], w_up, w_down, fin, bd)
    return (h_p.reshape(b, t, d), h_s.reshape(bd, s, d), *state)
```

```python
import functools

import jax
import jax.numpy as jnp
import numpy as np
from jax import lax
from jax.experimental import pallas as pl
from jax.experimental.pallas import tpu as pltpu

F32 = jnp.float32
BF16 = jnp.bfloat16
NEG_INF = float("-inf")

HEAD_DIM = 64
ROT_DIM = HEAD_DIM // 4
ROPE_THETA = 500000.0
NORM_EPS = 1e-6
NSA_HEADS = 16
NSA_KV = 2
NSA_HPG = NSA_HEADS // NSA_KV
CMP_BLOCK = 32
CMP_STRIDE = 16
SLC_BLOCK = 64
SLC_TOPK = 16
NSA_WINDOW = 512
DIL_PAIRS = ((128, 1), (512, 4), (2048, 16))
DIL_HEADS = 8
MLA_HEADS = 16
MLA_NOPE = 64
MLA_ROPE = 32
MLA_V = 64
MLA_KV_RANK = 256
MLA_Q_RANK = 384
MLA_THETA = 10000.0
MLA_SCALE = (MLA_NOPE + MLA_ROPE) ** -0.5

LANES = 128
VMEM_LIMIT = 56 * 1024 * 1024


def _cparams(sem):
    return pltpu.CompilerParams(dimension_semantics=sem, vmem_limit_bytes=VMEM_LIMIT)


def _rms(x, gain):
    y = x * lax.rsqrt(jnp.mean(x * x, axis=-1, keepdims=True) + NORM_EPS)
    return y * gain


def _rope(x, cos, sa, sb, half):
    return x * cos + pltpu.roll(x, LANES - half, 1) * sa + pltpu.roll(x, half, 1) * sb


def _div_p2(x, n):
    assert n > 0 and n & (n - 1) == 0
    return x >> (n.bit_length() - 1)


def _mod_p2(x, n):
    assert n > 0 and n & (n - 1) == 0
    return x & (n - 1)


def _split3(x):
    hi = x.astype(BF16)
    r1 = x - hi.astype(F32)
    mid = r1.astype(BF16)
    lo = (r1 - mid.astype(F32)).astype(BF16)
    return hi, mid, lo


def _dot3(x, e):
    hi, mid, lo = _split3(x)
    d = lambda a: jnp.dot(a, e, preferred_element_type=F32)
    return d(hi) + d(mid) + d(lo)


def _dot_nt(a, b):
    return lax.dot_general(a, b, (((1,), (1,)), ((), ())), preferred_element_type=F32)


def _masked_softmax(s, mask):
    s = jnp.where(mask, s, NEG_INF)
    m = jnp.max(s, axis=-1, keepdims=True)
    m = jnp.where(m > NEG_INF, m, 0.0)
    e = jnp.exp(s - m)
    l = jnp.sum(e, axis=-1, keepdims=True)
    p = e / jnp.maximum(l, 1e-30)
    return p, m, l


def _flash_tile(s, v_ones, m_ref, acc_ref):
    m_old = m_ref[...]
    m_new = jnp.maximum(m_old, jnp.max(s, axis=-1, keepdims=True))
    m_safe = jnp.where(m_new > NEG_INF, m_new, 0.0)
    alpha = jnp.exp(m_old - m_safe)
    p = jnp.exp(s - jnp.concatenate([m_safe] * (s.shape[1] // LANES), axis=1))
    acc_ref[...] = alpha * acc_ref[...] + jnp.dot(p.astype(BF16), v_ones, preferred_element_type=F32)
    m_ref[...] = m_new


def _flash_init(m_ref, acc_ref):
    m_ref[...] = jnp.full(m_ref.shape, NEG_INF, F32)
    acc_ref[...] = jnp.zeros(acc_ref.shape, F32)


def _flash_finish(acc):
    return acc / jnp.maximum(pltpu.roll(acc, HEAD_DIM, 1), 1e-30)


def _online_init(m_ref, l_ref, acc_ref):
    m_ref[...] = jnp.full(m_ref.shape, NEG_INF, F32)
    l_ref[...] = jnp.zeros(l_ref.shape, F32)
    acc_ref[...] = jnp.zeros(acc_ref.shape, F32)


def _rope_tables(pos, rot, theta, period, offset):
    half = rot // 2
    inv = theta ** (-(jnp.arange(half, dtype=F32) * 2.0) / rot)
    ang = pos.astype(F32)[:, None] * inv[None, :]
    cos, sin = jnp.cos(ang), jnp.sin(ang)
    lane = np.arange(LANES)
    i = lane % period - offset
    first = (i >= 0) & (i < half)
    second = (i >= half) & (i < rot)
    idx = np.where(first, i, np.where(second, i - half, 0))
    cos_t = jnp.where(jnp.asarray(first | second)[None, :], cos[:, idx], 1.0)
    sa_t = jnp.where(jnp.asarray(first)[None, :], -sin[:, idx], 0.0)
    sb_t = jnp.where(jnp.asarray(second)[None, :], sin[:, idx], 0.0)
    return cos_t, sa_t, sb_t


def _row_tile(m, pref):
    t = min(m, pref)
    while m % t:
        t //= 2
    return t


NSA_QW = NSA_HEADS * HEAD_DIM
NSA_KVW = 6 * NSA_KV * HEAD_DIM
NSA_NP = NSA_QW + NSA_KVW + LANES


def _nsa_weights(w_in, w_out):
    nq, nkv = NSA_QW, NSA_KVW
    perm = []
    for j in range(NSA_HPG):
        perm += list(range(j * HEAD_DIM, (j + 1) * HEAD_DIM))
        perm += list(range((NSA_HPG + j) * HEAD_DIM, (NSA_HPG + j + 1) * HEAD_DIM))
    perm = np.asarray(perm)
    ng = 3 * NSA_HEADS
    w = jnp.concatenate([w_in[:, :nq][:, perm], w_in[:, nq:nq + nkv], w_in[:, nq + nkv:],
                         jnp.zeros((w_in.shape[0], LANES - ng), w_in.dtype)], axis=1)
    return w.astype(BF16), w_out[perm, :].astype(BF16)


def _nsa_proj_body(x_ref, g_ref, w_ref, cos_ref, sa_ref, sb_ref, q_ref, gate_ref, emit_kv):
    xn = _rms(x_ref[...], g_ref[...]).astype(BF16)
    cos, sa, sb = cos_ref[...], sa_ref[...], sb_ref[...]
    half = ROT_DIM // 2
    nqb = NSA_QW // LANES
    nkb = NSA_KVW // LANES
    for c in range(0, nqb + nkb + 1, 2):
        nb = min(2, nqb + nkb + 1 - c)
        h = jnp.dot(xn, w_ref[:, c * LANES:(c + nb) * LANES], preferred_element_type=F32)
        for u in range(nb):
            blk = c + u
            hb = h[:, u * LANES:(u + 1) * LANES]
            if blk < nqb:
                q = _rope(hb, cos, sa, sb, half) * (HEAD_DIM ** -0.5)
                q_ref[:, blk * LANES:(blk + 1) * LANES] = q.astype(BF16)
            elif blk < nqb + nkb:
                kb = blk - nqb
                if kb % 2 == 0:
                    hb = _rope(hb, cos, sa, sb, half)
                emit_kv(kb, hb)
            else:
                gate_ref[...] = 1.0 / (1.0 + jnp.exp(-hb))


def _nsa_proj_kernel(x_ref, g_ref, w_ref, cos_ref, sa_ref, sb_ref, q_ref, kvf_ref, kvb_ref, gate_ref):
    def emit_kv(kb, hb):
        kvf_ref[:, kb * LANES:(kb + 1) * LANES] = hb
        kvb_ref[:, kb * LANES:(kb + 1) * LANES] = hb.astype(BF16)
    _nsa_proj_body(x_ref, g_ref, w_ref, cos_ref, sa_ref, sb_ref, q_ref, gate_ref, emit_kv)


def _nsa_proj_prompt_kernel(x_ref, g_ref, w_ref, cos_ref, sa_ref, sb_ref, cw_ref,
                            q_ref, kvb_ref, gate_ref, nsat_ref, wint_ref, fs_ref):
    cmp_rows = {}
    n_cache = 4

    def emit_kv(kb, hb):
        kvb_ref[:, kb * LANES:(kb + 1) * LANES] = hb.astype(BF16)
        ht = jnp.transpose(hb)
        if kb < n_cache:
            nsat_ref[kb * LANES:(kb + 1) * LANES, :] = ht
        else:
            wint_ref[(kb - n_cache) * LANES:(kb - n_cache + 1) * LANES, :] = ht
        if kb < 2:
            cmp_rows[kb] = hb
    _nsa_proj_body(x_ref, g_ref, w_ref, cos_ref, sa_ref, sb_ref, q_ref, gate_ref, emit_kv)
    first, second = _chunk_sums(jnp.concatenate([cmp_rows[0], cmp_rows[1]], axis=1), cw_ref[...])
    fs_ref[:, 0:2 * LANES] = first
    fs_ref[:, 2 * LANES:4 * LANES] = second


def _nsa_proj_prompt(x, gain, w, tabs, cw, b, t, tm):
    m, d = x.shape
    nt = t // tm
    row = lambda i: (i, 0)
    tab = lambda i: (i % nt, 0)
    const = lambda i: (0, 0)
    feat = lambda i: (i // nt, 0, i % nt)
    nc = tm // CMP_STRIDE
    return pl.pallas_call(
        _nsa_proj_prompt_kernel,
        grid=(m // tm,),
        in_specs=[pl.BlockSpec((tm, d), row), pl.BlockSpec((1, d), const), pl.BlockSpec(w.shape, const),
                  pl.BlockSpec((tm, LANES), tab), pl.BlockSpec((tm, LANES), tab), pl.BlockSpec((tm, LANES), tab),
                  pl.BlockSpec(cw.shape, const)],
        out_specs=[pl.BlockSpec((tm, NSA_QW), row), pl.BlockSpec((tm, NSA_KVW), row), pl.BlockSpec((tm, LANES), row),
                   pl.BlockSpec((None, 4 * LANES, tm), feat), pl.BlockSpec((None, 2 * LANES, tm), feat),
                   pl.BlockSpec((nc, 4 * LANES), row)],
        out_shape=[jax.ShapeDtypeStruct((m, NSA_QW), BF16), jax.ShapeDtypeStruct((m, NSA_KVW), BF16),
                   jax.ShapeDtypeStruct((m, LANES), F32), jax.ShapeDtypeStruct((b, 4 * LANES, t), F32),
                   jax.ShapeDtypeStruct((b, 2 * LANES, t), F32),
                   jax.ShapeDtypeStruct((m // CMP_STRIDE, 4 * LANES), F32)],
        compiler_params=_cparams(("parallel",)),
        name="nsa_proj_prompt",
    )(x, gain.reshape(1, d), w, *tabs, cw)


def _nsa_proj(x, gain, w, tabs, tab_blocks, tm):
    m, d = x.shape
    row = lambda i: (i, 0)
    tab = lambda i: (i % tab_blocks, 0)
    const = lambda i: (0, 0)
    return pl.pallas_call(
        _nsa_proj_kernel,
        grid=(m // tm,),
        in_specs=[pl.BlockSpec((tm, d), row), pl.BlockSpec((1, d), const), pl.BlockSpec(w.shape, const),
                  pl.BlockSpec((tm, LANES), tab), pl.BlockSpec((tm, LANES), tab), pl.BlockSpec((tm, LANES), tab)],
        out_specs=[pl.BlockSpec((tm, NSA_QW), row), pl.BlockSpec((tm, NSA_KVW), row),
                   pl.BlockSpec((tm, NSA_KVW), row), pl.BlockSpec((tm, LANES), row)],
        out_shape=[jax.ShapeDtypeStruct((m, NSA_QW), BF16), jax.ShapeDtypeStruct((m, NSA_KVW), F32),
                   jax.ShapeDtypeStruct((m, NSA_KVW), BF16), jax.ShapeDtypeStruct((m, LANES), F32)],
        compiler_params=_cparams(("parallel",)),
        name="nsa_proj",
    )(x, gain.reshape(1, d), w, *tabs)


def _cmp_weights(cmp_pos, cmp_proj):
    w = jnp.concatenate([cmp_pos[0], cmp_pos[0], cmp_pos[1], cmp_pos[1]], axis=-1)
    z = jnp.zeros((HEAD_DIM, HEAD_DIM), cmp_proj.dtype)
    rows = []
    for t in range(4):
        p = cmp_proj[t // 2]
        rows.append(jnp.concatenate([p if u == t else z for u in range(4)], axis=1))
    return w.astype(F32), jnp.concatenate(rows, axis=0).astype(BF16)


def _chunk_sums(x, w):
    n = x.shape[0] // CMP_STRIDE
    x3 = x.reshape(n, CMP_STRIDE, x.shape[1])
    first = jnp.sum(x3 * w[:CMP_STRIDE][None], axis=1)
    second = jnp.sum(x3 * w[CMP_STRIDE:][None], axis=1)
    return first, second


def _cmp_prompt_kernel(fs_ref, proj_ref, o_ref):
    n = fs_ref.shape[0]
    nxt = pltpu.roll(fs_ref[:, 2 * LANES:4 * LANES], n - 1, 0)
    ridx = lax.broadcasted_iota(jnp.int32, (n, 2 * LANES), 0)
    s = jnp.where(ridx < n - 1, fs_ref[:, 0:2 * LANES] + nxt, 0.0)
    o_ref[...] = jnp.dot(s.astype(BF16), proj_ref[...], preferred_element_type=F32).astype(BF16)


def _cmp_prompt(fs, proj, b, t):
    n = t // CMP_STRIDE
    return pl.pallas_call(
        _cmp_prompt_kernel,
        grid=(b,),
        in_specs=[pl.BlockSpec((None, n, 4 * LANES), lambda i: (i, 0, 0)), pl.BlockSpec(proj.shape, lambda i: (0, 0))],
        out_specs=pl.BlockSpec((None, n, 2 * LANES), lambda i: (i, 0, 0)),
        out_shape=jax.ShapeDtypeStruct((b, n, 2 * LANES), BF16),
        compiler_params=_cparams(("parallel",)),
        name="nsa_cmp_blocks",
    )(fs.reshape(b, n, 4 * LANES), proj)


def _select_matrix(n_cmp_pad, n_sel):
    i = np.arange(n_cmp_pad)[:, None]
    j = np.arange(LANES if n_sel <= LANES else 2 * LANES)[None, :]
    s = ((i // 4 == j) | (i == 4 * j - 1)) & (j < n_sel)
    return jnp.asarray(s, BF16)


def _topk_mask(sc, n_sel, k):
    lane = lax.broadcasted_iota(jnp.int32, sc.shape, 1)
    rank = jnp.zeros(sc.shape, F32)
    for i in range(n_sel):
        col = sc[:, i:i + 1]
        beats = (col > sc) | ((col == sc) & (lane > i))
        rank = rank + jnp.where(beats, 1.0, 0.0)
    return jnp.where((rank < k) & (lane < n_sel), 1.0, 0.0)


def _rank_select_t(sc_t, n_sel, k):
    blk = lax.broadcasted_iota(jnp.int32, sc_t.shape, 0)
    rank = jnp.zeros(sc_t.shape, F32)
    for i in range(n_sel):
        row = sc_t[i:i + 1, :]
        beats = (row > sc_t) | ((row == sc_t) & (blk > i))
        rank = rank + jnp.where(beats, 1.0, 0.0)
    return jnp.where((rank < k) & (blk < n_sel), 1.0, 0.0)


def _nsa_attn2_kernel(q_ref, kv_ref, ckv_ref, gate_ref, smat_ref, o_ref,
                      qs_ref, sel_ref, bias_ref, m_ref, acc_ref, oc_ref, os_ref, ow_ref,
                      *, tq, tk, wk, n_sel):
    i = pl.program_id(1)
    q0 = i * tq
    hq = NSA_HPG * tq
    lane1 = lax.broadcasted_iota(jnp.int32, (1, LANES), 1)
    low = lane1 < HEAD_DIM

    for j in range(NSA_HPG):
        q2 = q_ref[:, j * LANES:(j + 1) * LANES]
        zero = jnp.zeros_like(q2)
        qs_ref[j * tq:(j + 1) * tq] = jnp.where(low, q2, zero)
        qs_ref[(NSA_HPG + j) * tq:(NSA_HPG + j + 1) * tq] = jnp.where(low, zero, q2)

    nc = ckv_ref.shape[0]
    ck2 = ckv_ref[:, 0:LANES]
    cv2 = ckv_ref[:, LANES:2 * LANES]
    qp_c = q0 + lax.broadcasted_iota(jnp.int32, (tq, nc), 0)
    n_c = lax.broadcasted_iota(jnp.int32, (tq, nc), 1)
    cbias = jnp.where((n_c * CMP_STRIDE + CMP_BLOCK - 1) <= qp_c, 0.0, NEG_INF)
    qp2 = q0 + lax.broadcasted_iota(jnp.int32, (tq, LANES), 0)
    blk2 = lax.broadcasted_iota(jnp.int32, (tq, LANES), 1)
    cur = qp2 >> 6
    forced = (blk2 == 0) | (blk2 == cur) | (blk2 == cur - 1)
    future = (blk2 * SLC_BLOCK) > qp2
    nsp = _round_up(n_sel, 8)
    for g in range(NSA_KV):
        rows = slice(g * hq, (g + 1) * hq)
        s = _dot_nt(qs_ref[rows], ck2).reshape(NSA_HPG, tq, nc) + cbias[None]
        m = jnp.max(s, axis=-1, keepdims=True)
        m = jnp.where(m > NEG_INF, m, 0.0)
        e = jnp.exp(s - m)
        p = (e / jnp.maximum(jnp.sum(e, axis=-1, keepdims=True), 1e-30)).reshape(hq, nc)
        oc_ref[rows] = jnp.dot(p.astype(BF16), cv2, preferred_element_type=F32)
        imp = jnp.sum(p.reshape(NSA_HPG, tq, nc), axis=0)
        sc = _dot3(imp, smat_ref[...])
        sc = jnp.where(forced, jnp.inf, jnp.where(future, NEG_INF, sc))
        sc = jnp.where(blk2 < n_sel, sc, NEG_INF)
        sel_t = _rank_select_t(jnp.transpose(sc)[0:nsp], n_sel, min(SLC_TOPK, n_sel))
        sel_t = jnp.concatenate([sel_t, jnp.zeros((LANES - nsp, tq), F32)], axis=0)
        sel_ref[g] = jnp.transpose(sel_t).astype(BF16)

    n_kt = (q0 + tq + tk - 1) // tk
    qp_k = q0 + lax.broadcasted_iota(jnp.int32, (tq, tk), 0)
    col_k = lax.broadcasted_iota(jnp.int32, (tq, tk), 1)
    eb_r = lax.broadcasted_iota(jnp.int32, (LANES, tk), 0)
    eb_c = lax.broadcasted_iota(jnp.int32, (LANES, tk), 1)
    for g in range(NSA_KV):
        rows = slice(g * hq, (g + 1) * hq)
        mine = (lane1 >= g * HEAD_DIM) & (lane1 < (g + 1) * HEAD_DIM)

        def bias_body(t, carry, g=g):
            k0 = t * tk
            expand = jnp.where(eb_r == ((k0 + eb_c) >> 6), 1.0, 0.0).astype(BF16)
            selx = jnp.dot(sel_ref[g], expand, preferred_element_type=F32)
            ok = (selx > 0.5) & ((k0 + col_k) <= qp_k)
            bias_ref[t] = jnp.where(ok, 0.0, NEG_INF)
            return carry

        lax.fori_loop(0, n_kt, bias_body, 0)
        _flash_init(m_ref, acc_ref)

        def sel_body(t, carry, rows=rows, mine=mine):
            k0 = pl.multiple_of(t * tk, tk)
            k2 = kv_ref[pl.ds(k0, tk), 2 * LANES:3 * LANES]
            v2 = kv_ref[pl.ds(k0, tk), 3 * LANES:4 * LANES]
            v_ones = jnp.where(mine, v2, jnp.ones_like(v2))
            s = _dot_nt(qs_ref[rows], k2).reshape(NSA_HPG, tq, tk) + bias_ref[t][None]
            _flash_tile(s.reshape(hq, tk), v_ones, m_ref, acc_ref)
            return carry

        lax.fori_loop(0, n_kt, sel_body, 0)
        os_ref[rows] = _flash_finish(acc_ref[...])

    ws = pl.multiple_of(jnp.maximum(q0 + tq - wk, 0), LANES)
    dist = (q0 + lax.broadcasted_iota(jnp.int32, (tq, wk), 0)) - (ws + lax.broadcasted_iota(jnp.int32, (tq, wk), 1))
    wbias = jnp.where((dist >= 0) & (dist <= NSA_WINDOW), 0.0, NEG_INF)
    for g in range(NSA_KV):
        rows = slice(g * hq, (g + 1) * hq)
        mine = (lane1 >= g * HEAD_DIM) & (lane1 < (g + 1) * HEAD_DIM)
        k2 = kv_ref[pl.ds(ws, wk), 4 * LANES:5 * LANES]
        v2 = kv_ref[pl.ds(ws, wk), 5 * LANES:6 * LANES]
        v_ones = jnp.where(mine, v2, jnp.ones_like(v2))
        s = _dot_nt(qs_ref[rows], k2).reshape(NSA_HPG, tq, wk) + wbias[None]
        m = jnp.max(s, axis=-1, keepdims=True)
        m = jnp.where(m > NEG_INF, m, 0.0)
        p = jnp.exp(s - m).reshape(hq, wk)
        ow_ref[rows] = _flash_finish(jnp.dot(p.astype(BF16), v_ones, preferred_element_type=F32))

    branch = (oc_ref, os_ref, ow_ref)
    for j in range(NSA_HPG):
        parts = []
        for g in range(NSA_KV):
            h = g * NSA_HPG + j
            r = slice(h * tq, (h + 1) * tq)
            x = jnp.zeros((tq, LANES), F32)
            for br in range(3):
                c = br * NSA_HEADS + h
                x = x + gate_ref[:, c:c + 1] * branch[br][r]
            parts.append(x)
        o_ref[:, j * LANES:(j + 1) * LANES] = jnp.where(low, parts[0], parts[1]).astype(BF16)


def _nsa_attn2(q, kvb, ckv, gates, b, t, tq=128, tk=512):
    tk = min(tk, t)
    n_sel = t // SLC_BLOCK
    nc = ckv.shape[1]
    assert nc % LANES == 0 and n_sel <= LANES and t % tk == 0
    wk = min(NSA_WINDOW + tq, t)
    smat = _select_matrix(nc, n_sel)
    hq = NSA_HPG * tq
    kern = functools.partial(_nsa_attn2_kernel, tq=tq, tk=tk, wk=wk, n_sel=n_sel)
    return pl.pallas_call(
        kern,
        grid=(b, t // tq),
        in_specs=[pl.BlockSpec((None, tq, NSA_QW), lambda bi, i: (bi, i, 0)),
                  pl.BlockSpec((None, t, NSA_KVW), lambda bi, i: (bi, 0, 0)),
                  pl.BlockSpec((None, nc, 2 * LANES), lambda bi, i: (bi, 0, 0)),
                  pl.BlockSpec((None, tq, LANES), lambda bi, i: (bi, i, 0)),
                  pl.BlockSpec(smat.shape, lambda bi, i: (0, 0))],
        out_specs=pl.BlockSpec((None, tq, NSA_QW), lambda bi, i: (bi, i, 0)),
        out_shape=jax.ShapeDtypeStruct((b, t, NSA_QW), BF16),
        scratch_shapes=[pltpu.VMEM((NSA_HEADS * tq, LANES), BF16), pltpu.VMEM((NSA_KV, tq, LANES), BF16),
                        pltpu.VMEM((t // tk, tq, tk), F32),
                        pltpu.VMEM((hq, LANES), F32), pltpu.VMEM((hq, LANES), F32),
                        pltpu.VMEM((NSA_HEADS * tq, LANES), F32), pltpu.VMEM((NSA_HEADS * tq, LANES), F32),
                        pltpu.VMEM((NSA_HEADS * tq, LANES), F32)],
        compiler_params=_cparams(("parallel", "parallel")),
        name="nsa_attn",
    )(q.reshape(b, t, NSA_QW), kvb.reshape(b, t, NSA_KVW), ckv, gates.reshape(b, t, LANES), smat)


def _out_proj_kernel(o_ref, w_ref, r_ref, y_ref):
    y_ref[...] = r_ref[...] + jnp.dot(o_ref[...], w_ref[...], preferred_element_type=F32)


def _out_proj(o, w, res, tm):
    m, k = o.shape
    d = w.shape[1]
    return pl.pallas_call(
        _out_proj_kernel,
        grid=(m // tm,),
        in_specs=[pl.BlockSpec((tm, k), lambda i: (i, 0)), pl.BlockSpec(w.shape, lambda i: (0, 0)),
                  pl.BlockSpec((tm, d), lambda i: (i, 0))],
        out_specs=pl.BlockSpec((tm, d), lambda i: (i, 0)),
        out_shape=jax.ShapeDtypeStruct((m, d), F32),
        compiler_params=_cparams(("parallel",)),
        name="out_proj",
    )(o, w, res)


def _mlp_kernel(x_ref, g_ref, wu_ref, wd_ref, fg_ref, y_ref, xn_ref, acc_ref, *, final_norm):
    j = pl.program_id(1)

    @pl.when(j == 0)
    def _():
        xn_ref[...] = _rms(x_ref[...], g_ref[...]).astype(BF16)
        acc_ref[...] = x_ref[...]

    h = jnp.maximum(jnp.dot(xn_ref[...], wu_ref[...], preferred_element_type=F32), 0.0)
    acc_ref[...] += jnp.dot((h * h).astype(BF16), wd_ref[...], preferred_element_type=F32)

    @pl.when(j == pl.num_programs(1) - 1)
    def _():
        y = acc_ref[...]
        if final_norm:
            y = _rms(y, fg_ref[...])
        y_ref[...] = y


def _mlp(x, gain, w_up, w_down, final_gain, tm, tf=1024):
    m, d = x.shape
    ff = w_up.shape[1]
    final_norm = final_gain is not None
    fg = (final_gain if final_norm else gain).reshape(1, d)
    kern = functools.partial(_mlp_kernel, final_norm=final_norm)
    return pl.pallas_call(
        kern,
        grid=(m // tm, ff // tf),
        in_specs=[pl.BlockSpec((tm, d), lambda i, j: (i, 0)), pl.BlockSpec((1, d), lambda i, j: (0, 0)),
                  pl.BlockSpec((d, tf), lambda i, j: (0, j)), pl.BlockSpec((tf, d), lambda i, j: (j, 0)),
                  pl.BlockSpec((1, d), lambda i, j: (0, 0))],
        out_specs=pl.BlockSpec((tm, d), lambda i, j: (i, 0)),
        out_shape=jax.ShapeDtypeStruct((m, d), F32),
        scratch_shapes=[pltpu.VMEM((tm, d), BF16), pltpu.VMEM((tm, d), F32)],
        compiler_params=_cparams(("parallel", "arbitrary")),
        name="mlp",
    )(x, gain.reshape(1, d), w_up, w_down, fg)


DIL_GW = DIL_HEADS * HEAD_DIM
DIL_NG = len(DIL_PAIRS)


def _dil_proj_kernel(x_ref, g_ref, w_ref, cos_ref, sa_ref, sb_ref, hb_ref, st_ref, xn_ref):
    j = pl.program_id(1)

    @pl.when(j == 0)
    def _():
        xn_ref[...] = _rms(x_ref[...], g_ref[...]).astype(BF16)

    cos, sa, sb = cos_ref[...], sa_ref[...], sb_ref[...]
    per = DIL_GW // LANES
    for c in range(3):
        h = jnp.dot(xn_ref[...], w_ref[:, c * DIL_GW:(c + 1) * DIL_GW], preferred_element_type=F32)
        for u in range(per):
            r = h[:, u * LANES:(u + 1) * LANES]
            if c < 2:
                r = _rope(r, cos, sa, sb, ROT_DIM // 2)
            if c > 0:
                st_ref[:, ((c - 1) * per + u) * LANES:((c - 1) * per + u + 1) * LANES] = r
            else:
                r = r * (HEAD_DIM ** -0.5)
            hb_ref[:, (c * per + u) * LANES:(c * per + u + 1) * LANES] = r.astype(BF16)


def _dil_proj(x, gain, w, tabs, tab_blocks, tm):
    m, d = x.shape
    n = w.shape[1]
    gw = 3 * DIL_GW
    tab = lambda i, j: (i % tab_blocks, 0)
    st_spec = pl.BlockSpec((tm, 2 * DIL_GW), lambda i, j: (i, j))
    st_shape = jax.ShapeDtypeStruct((m, 2 * DIL_NG * DIL_GW), F32)
    return pl.pallas_call(
        _dil_proj_kernel,
        grid=(m // tm, n // gw),
        in_specs=[pl.BlockSpec((tm, d), lambda i, j: (i, 0)), pl.BlockSpec((1, d), lambda i, j: (0, 0)),
                  pl.BlockSpec((d, gw), lambda i, j: (0, j)),
                  pl.BlockSpec((tm, LANES), tab), pl.BlockSpec((tm, LANES), tab), pl.BlockSpec((tm, LANES), tab)],
        out_specs=[pl.BlockSpec((tm, gw), lambda i, j: (i, j)), st_spec],
        out_shape=[jax.ShapeDtypeStruct((m, n), BF16), st_shape],
        scratch_shapes=[pltpu.VMEM((tm, d), BF16)],
        compiler_params=_cparams(("parallel", "arbitrary")),
        name="dil_proj",
    )(x, gain.reshape(1, d), w, *tabs)


def _dil_proj_prompt_kernel(x_ref, g_ref, w_ref, cos_ref, sa_ref, sb_ref,
                            hb0_ref, hb1_ref, hb2_ref, st_ref, xn_ref, il_ref):
    j = pl.program_id(1)
    tm = x_ref.shape[0]

    @pl.when(j == 0)
    def _():
        xn_ref[...] = _rms(x_ref[...], g_ref[...]).astype(BF16)

    per = DIL_GW // LANES
    hbs = (hb0_ref, hb1_ref, hb2_ref)
    for g, (_, r) in enumerate(DIL_PAIRS):
        @pl.when(j == g)
        def _(g=g, r=r):
            cos, sa, sb = cos_ref[...], sa_ref[...], sb_ref[...]
            for c in range(3):
                h = jnp.dot(xn_ref[...], w_ref[:, c * DIL_GW:(c + 1) * DIL_GW], preferred_element_type=F32)
                for u in range(per):
                    v = h[:, u * LANES:(u + 1) * LANES]
                    if c < 2:
                        v = _rope(v, cos, sa, sb, ROT_DIM // 2)
                    if c > 0:
                        st_ref[((c - 1) * per + u) * LANES:((c - 1) * per + u + 1) * LANES, :] = jnp.transpose(v)
                    else:
                        v = v * (HEAD_DIM ** -0.5)
                    lanes = slice((c * per + u) * LANES, (c * per + u + 1) * LANES)
                    if r == 1:
                        hbs[g][0, :, lanes] = v.astype(BF16)
                    else:
                        il_ref[...] = v
                        for rho in range(r):
                            hbs[g][rho, :, lanes] = il_ref[pl.ds(rho, tm // r, stride=r), :].astype(BF16)


def _dil_proj_prompt(x, gain, w, tabs, b, t, tm):
    m, d = x.shape
    gw = 3 * DIL_GW
    nt = t // tm
    tab = lambda i, j: (i % nt, 0)
    hb_specs, hb_shapes = [], []
    for _, r in DIL_PAIRS:
        assert tm % (16 * r) == 0
        hb_specs.append(pl.BlockSpec((None, r, tm // r, gw), lambda i, j: (i // nt, 0, i % nt, 0)))
        hb_shapes.append(jax.ShapeDtypeStruct((b, r, t // r, gw), BF16))
    return pl.pallas_call(
        _dil_proj_prompt_kernel,
        grid=(m // tm, DIL_NG),
        in_specs=[pl.BlockSpec((tm, d), lambda i, j: (i, 0)), pl.BlockSpec((1, d), lambda i, j: (0, 0)),
                  pl.BlockSpec((d, gw), lambda i, j: (0, j)),
                  pl.BlockSpec((tm, LANES), tab), pl.BlockSpec((tm, LANES), tab), pl.BlockSpec((tm, LANES), tab)],
        out_specs=hb_specs + [pl.BlockSpec((None, None, 2 * DIL_GW, tm), lambda i, j: (i // nt, j, 0, i % nt))],
        out_shape=hb_shapes + [jax.ShapeDtypeStruct((b, DIL_NG, 2 * DIL_GW, t), F32)],
        scratch_shapes=[pltpu.VMEM((tm, d), BF16), pltpu.VMEM((tm, LANES), F32)],
        compiler_params=_cparams(("parallel", "arbitrary")),
        name="dil_proj_prompt",
    )(x, gain.reshape(1, d), w, *tabs)


def _dil_attn_kernel(q_ref, k_ref, v_ref, o_ref, lse_ref, *, tq, win, ls):
    i = pl.program_id(2)
    nk = min(2 * tq, ls)
    start = pl.multiple_of(jnp.clip((i - 1) * tq, 0, ls - nk), tq)
    low = lax.broadcasted_iota(jnp.int32, (1, LANES), 1) < HEAD_DIM
    qp = i * tq + _mod_p2(lax.broadcasted_iota(jnp.int32, (2 * tq, nk), 0), tq)
    kp = start + lax.broadcasted_iota(jnp.int32, (2 * tq, nk), 1)
    dist = qp - kp
    mask = (dist >= 0) & (dist <= win)
    for jp in range(DIL_GW // LANES):
        cols = slice(jp * LANES, (jp + 1) * LANES)
        q2 = q_ref[:, cols]
        zero = jnp.zeros_like(q2)
        qs = jnp.concatenate([jnp.where(low, q2, zero), jnp.where(low, zero, q2)], axis=0)
        k2 = k_ref[pl.ds(start, nk), cols]
        v2 = v_ref[pl.ds(start, nk), cols]
        p, m, l = _masked_softmax(_dot_nt(qs, k2), mask)
        o2 = jnp.dot(p.astype(BF16), v2, preferred_element_type=F32)
        lse = m + jnp.log(jnp.maximum(l, 1e-30))
        o_ref[:, cols] = jnp.where(low, o2[:tq], o2[tq:])
        lse_ref[:, cols] = jnp.where(low, jnp.broadcast_to(lse[:tq], (tq, LANES)),
                                     jnp.broadcast_to(lse[tq:], (tq, LANES)))


def _dil_attn(hv, b, t, span, r, tq=128):
    ls = t // r
    kern = functools.partial(_dil_attn_kernel, tq=tq, win=span // r, ls=ls)
    o, lse = pl.pallas_call(
        kern,
        grid=(b, r, ls // tq),
        in_specs=[pl.BlockSpec((None, None, tq, DIL_GW), lambda bi, rho, i: (bi, rho, i, 0)),
                  pl.BlockSpec((None, None, ls, DIL_GW), lambda bi, rho, i: (bi, rho, 0, 1)),
                  pl.BlockSpec((None, None, ls, DIL_GW), lambda bi, rho, i: (bi, rho, 0, 2))],
        out_specs=[pl.BlockSpec((None, tq, DIL_GW), lambda bi, rho, i: (bi, i, rho)),
                   pl.BlockSpec((None, tq, DIL_GW), lambda bi, rho, i: (bi, i, rho))],
        out_shape=[jax.ShapeDtypeStruct((b, ls, r * DIL_GW), F32), jax.ShapeDtypeStruct((b, ls, r * DIL_GW), F32)],
        compiler_params=_cparams(("parallel", "parallel", "parallel")),
        name=f"dil_attn_{span}",
    )(hv, hv, hv)
    return o.reshape(b * t, DIL_GW), lse.reshape(b * t, DIL_GW)


def _dil_out_kernel(o0_ref, o1_ref, o2_ref, l0_ref, l1_ref, l2_ref, w_ref, r_ref, y_ref):
    ls = [l0_ref[...], l1_ref[...], l2_ref[...]]
    mx = jnp.maximum(jnp.maximum(ls[0], ls[1]), ls[2])
    es = [jnp.exp(l - mx) for l in ls]
    den = es[0] + es[1] + es[2]
    o = (es[0] / den) * o0_ref[...] + (es[1] / den) * o1_ref[...] + (es[2] / den) * o2_ref[...]
    y_ref[...] = r_ref[...] + jnp.dot(o.astype(BF16), w_ref[...], preferred_element_type=F32)


def _dil_out(os, lses, w, res, tm):
    m, d = res.shape
    row = lambda i: (i, 0)
    return pl.pallas_call(
        _dil_out_kernel,
        grid=(m // tm,),
        in_specs=[pl.BlockSpec((tm, DIL_GW), row)] * 6 + [pl.BlockSpec(w.shape, lambda i: (0, 0)),
                                                         pl.BlockSpec((tm, d), row)],
        out_specs=pl.BlockSpec((tm, d), row),
        out_shape=jax.ShapeDtypeStruct((m, d), F32),
        compiler_params=_cparams(("parallel",)),
        name="dil_out",
    )(*os, *lses, w, res)


MLA_QCW = MLA_HEADS * LANES
MLA_INP = 768
MLA_ST = MLA_KV_RANK + MLA_ROPE


def _mla_weights(w_in, w_uq, w_ukv):
    d = w_in.shape[0]
    n_in = MLA_Q_RANK + MLA_KV_RANK + MLA_ROPE
    win = jnp.concatenate([w_in, jnp.zeros((d, MLA_INP - n_in), w_in.dtype)], axis=1)
    uq = w_uq.reshape(MLA_Q_RANK, MLA_HEADS, MLA_NOPE + MLA_ROPE)
    uq = jnp.concatenate([uq, jnp.zeros((MLA_Q_RANK, MLA_HEADS, LANES - MLA_NOPE - MLA_ROPE), uq.dtype)], axis=2)
    ukv = w_ukv.reshape(MLA_KV_RANK, MLA_HEADS, MLA_NOPE + MLA_V)
    wk = jnp.concatenate([ukv[..., :MLA_NOPE], jnp.zeros((MLA_KV_RANK, MLA_HEADS, LANES - MLA_NOPE), ukv.dtype)],
                         axis=2)
    wv = ukv[..., MLA_NOPE:]
    return (win.astype(BF16), uq.reshape(MLA_Q_RANK, MLA_QCW).astype(BF16),
            wk.reshape(MLA_KV_RANK, MLA_QCW).astype(BF16), wv.reshape(MLA_KV_RANK, MLA_HEADS * MLA_V).astype(BF16))


def _mla_proj_kernel(x_ref, g_ref, win_ref, qg_ref, kvg_ref, wuq_ref, wk_ref, wv_ref,
                     qc_ref, qa_ref, qb_ref, kc_ref, ka_ref, kb_ref, q_ref, k_ref, v_ref, st_ref, *, feature_major):
    xn = _rms(x_ref[...], g_ref[...]).astype(BF16)
    h = jnp.dot(xn, win_ref[...], preferred_element_type=F32)
    cq = _rms(h[:, 0:MLA_Q_RANK], qg_ref[...]).astype(BF16)
    ckv = _rms(h[:, MLA_Q_RANK:MLA_Q_RANK + MLA_KV_RANK], kvg_ref[...])
    half = MLA_ROPE // 2
    kpe = _rope(h[:, MLA_Q_RANK + MLA_KV_RANK:MLA_INP], kc_ref[...], ka_ref[...], kb_ref[...], half)
    if feature_major:
        st_ref[0:MLA_KV_RANK, :] = jnp.transpose(ckv)
        st_ref[MLA_KV_RANK:MLA_ST, :] = jnp.transpose(kpe)[0:MLA_ROPE]
    else:
        st_ref[:, 0:MLA_KV_RANK] = ckv
        st_ref[:, MLA_KV_RANK:MLA_ST] = kpe[:, 0:MLA_ROPE]
    kpe_hi = pltpu.roll(kpe, MLA_NOPE, 1)
    ckv_b = ckv.astype(BF16)
    qc, qa, qb = qc_ref[...], qa_ref[...], qb_ref[...]
    for c in range(0, MLA_HEADS, 2):
        cols = slice(c * LANES, (c + 2) * LANES)
        qh = jnp.dot(cq, wuq_ref[:, cols], preferred_element_type=F32)
        kh = jnp.dot(ckv_b, wk_ref[:, cols], preferred_element_type=F32)
        for u in range(2):
            one = slice((c + u) * LANES, (c + u + 1) * LANES)
            q_ref[:, one] = _rope(qh[:, u * LANES:(u + 1) * LANES], qc, qa, qb, half).astype(BF16)
            k_ref[:, one] = (kh[:, u * LANES:(u + 1) * LANES] + kpe_hi).astype(BF16)
    v_ref[...] = jnp.dot(ckv_b, wv_ref[...], preferred_element_type=F32).astype(BF16)


def _mla_proj(x, p, ws, qtabs, ktabs, tab_blocks, tm, prompt_batch=None):
    m, d = x.shape
    win, wuq, wk, wv = ws
    row = lambda i: (i, 0)
    tab = lambda i: (i % tab_blocks, 0)
    const = lambda i: (0, 0)
    full = lambda a: pl.BlockSpec(a.shape, const)
    qg = p['q_norm'].reshape(1, -1)
    kvg = p['kv_norm'].reshape(1, -1)
    if prompt_batch is None:
        st_spec = pl.BlockSpec((tm, MLA_ST), row)
        st_shape = jax.ShapeDtypeStruct((m, MLA_ST), F32)
    else:
        nt = m // prompt_batch // tm
        st_spec = pl.BlockSpec((None, MLA_ST, tm), lambda i: (i // nt, 0, i % nt))
        st_shape = jax.ShapeDtypeStruct((prompt_batch, MLA_ST, m // prompt_batch), F32)
    return pl.pallas_call(
        functools.partial(_mla_proj_kernel, feature_major=prompt_batch is not None),
        grid=(m // tm,),
        in_specs=[pl.BlockSpec((tm, d), row), pl.BlockSpec((1, d), const), full(win), full(qg), full(kvg),
                  full(wuq), full(wk), full(wv)] + [pl.BlockSpec((tm, LANES), tab)] * 6,
        out_specs=[pl.BlockSpec((tm, MLA_QCW), row), pl.BlockSpec((tm, MLA_QCW), row),
                   pl.BlockSpec((tm, MLA_HEADS * MLA_V), row), st_spec],
        out_shape=[jax.ShapeDtypeStruct((m, MLA_QCW), BF16), jax.ShapeDtypeStruct((m, MLA_QCW), BF16),
                   jax.ShapeDtypeStruct((m, MLA_HEADS * MLA_V), BF16), st_shape],
        compiler_params=_cparams(("parallel",)),
        name="mla_proj",
    )(x, p['norm_attn'].reshape(1, d), win, qg, kvg, wuq, wk, wv, *qtabs, *ktabs)


def _mla_attn_kernel(q_ref, k_ref, v_ref, o_ref, m_ref, acc_ref, *, tq):
    i = pl.program_id(2)
    row = lax.broadcasted_iota(jnp.int32, (tq, tq), 0)
    col = lax.broadcasted_iota(jnp.int32, (tq, tq), 1)
    dbias = jnp.where(col <= row, 0.0, NEG_INF)
    low = lax.broadcasted_iota(jnp.int32, (1, LANES), 1) < MLA_V
    outs = []
    for hh in range(2):
        cols = slice(hh * LANES, (hh + 1) * LANES)
        mine = low if hh == 0 else jnp.logical_not(low)
        _flash_init(m_ref, acc_ref)

        def tile(t, bias, cols=cols, mine=mine):
            k0 = pl.multiple_of(t * tq, tq)
            s = _dot_nt(q_ref[:, cols], k_ref[pl.ds(k0, tq), cols]) * MLA_SCALE
            if bias is not None:
                s = s + bias
            v2 = v_ref[pl.ds(k0, tq), :]
            _flash_tile(s, jnp.where(mine, v2, jnp.ones_like(v2)), m_ref, acc_ref)

        def body(t, carry, tile=tile):
            tile(t, None)
            return carry

        lax.fori_loop(0, i, body, 0)
        tile(i, dbias)
        outs.append(_flash_finish(acc_ref[...]))
    o_ref[...] = jnp.where(low, outs[0], outs[1]).astype(BF16)


def _mla_attn(q, k, v, b, t, tq=512):
    tq = min(tq, t)
    kern = functools.partial(_mla_attn_kernel, tq=tq)
    return pl.pallas_call(
        kern,
        grid=(b, MLA_HEADS // 2, t // tq),
        in_specs=[pl.BlockSpec((None, tq, 2 * LANES), lambda bi, hp, i: (bi, i, hp)),
                  pl.BlockSpec((None, t, 2 * LANES), lambda bi, hp, i: (bi, 0, hp)),
                  pl.BlockSpec((None, t, LANES), lambda bi, hp, i: (bi, 0, hp))],
        out_specs=pl.BlockSpec((None, tq, LANES), lambda bi, hp, i: (bi, i, hp)),
        out_shape=jax.ShapeDtypeStruct((b, t, MLA_HEADS * MLA_V), BF16),
        scratch_shapes=[pltpu.VMEM((tq, LANES), F32), pltpu.VMEM((tq, LANES), F32)],
        compiler_params=_cparams(("parallel", "parallel", "parallel")),
        name="mla_attn",
    )(q.reshape(b, t, MLA_QCW), k.reshape(b, t, MLA_QCW), v.reshape(b, t, MLA_HEADS * MLA_V))


def _round_up(x, m):
    return (x + m - 1) // m * m


def _pool_feature_major(pool):
    n_phys, page = pool.shape[0], pool.shape[1]
    return jnp.transpose(pool, (0, 2, 3, 4, 1)).reshape(n_phys, 4 * NSA_KV * HEAD_DIM, page)


def _shift_rows_left(src_ref, dst_ref, new_row):
    f, w = src_ref.shape
    nt = w // LANES
    lane = lax.broadcasted_iota(jnp.int32, (f, LANES), 1)
    new_col = jnp.transpose(jnp.broadcast_to(new_row, (LANES, f)))
    nxt = pltpu.roll(src_ref[:, 0:LANES], LANES - 1, 1)
    for j in range(nt):
        cur = nxt
        if j + 1 < nt:
            nxt = pltpu.roll(src_ref[:, (j + 1) * LANES:(j + 2) * LANES], LANES - 1, 1)
            fill = nxt
        else:
            fill = new_col
        dst_ref[:, j * LANES:(j + 1) * LANES] = jnp.where(lane < LANES - 1, cur, fill)


def _stack_heads(q8):
    low = lax.broadcasted_iota(jnp.int32, (1, LANES), 1) < HEAD_DIM
    qf = q8.astype(F32)
    zero = jnp.zeros_like(qf)
    return jnp.concatenate([jnp.where(low, qf, zero), jnp.where(low, zero, qf)], axis=0).astype(BF16)


def _nsa_dec_cmp_kernel(pt_ref, q_ref, new_ref, pool_ref, w_ref, proj_ref, smat_ref, tri_ref,
                        oc_ref, idx_ref, cbuf, f_ref, s_ref, sem, *, n_pages, page, past, n_sel):
    b = pl.program_id(0)
    nb = pl.num_programs(0)
    slot = b % 2
    width = 2 * LANES
    new_rows = SLC_BLOCK
    n_chunks = (past + new_rows) // CMP_STRIDE
    n_cmp = n_chunks - 1
    ncp = f_ref.shape[0]

    def fetch(bb, sl):
        def body(pg, c):
            return pltpu.make_async_copy(pool_ref.at[pt_ref[bb, pg], pl.ds(0, width), :],
                                         cbuf.at[sl, pg], sem.at[sl])
        return body

    def start(bb, sl):
        mk = fetch(bb, sl)

        def body(pg, c):
            mk(pg, c).start()
            return c
        lax.fori_loop(0, n_pages, body, 0)

    @pl.when(b == 0)
    def _():
        start(0, 0)

    @pl.when(b + 1 < nb)
    def _():
        start(b + 1, 1 - slot)

    mk = fetch(b, slot)

    def wait_body(pg, c):
        mk(pg, c).wait()
        return c
    lax.fori_loop(0, n_pages, wait_body, 0)

    w = w_ref[...]
    per = page // CMP_STRIDE

    unroll = 4 if n_pages % 4 == 0 else 1

    def sum_body(c, carry):
        for u in range(unroll):
            pg = c * unroll + u
            first, second = _chunk_sums(cbuf[slot, pg].T, w)
            f_ref[pl.ds(pl.multiple_of(pg * per, per), per), :] = first
            s_ref[pl.ds(pl.multiple_of(pg * per, per), per), :] = second
        return carry
    lax.fori_loop(0, n_pages // unroll, sum_body, 0)
    tail0 = past // CMP_STRIDE
    f_ref[tail0:ncp, :] = jnp.zeros((ncp - tail0, width), F32)
    s_ref[tail0:ncp, :] = jnp.zeros((ncp - tail0, width), F32)
    ridx = lax.broadcasted_iota(jnp.int32, (new_rows, width), 0)
    first, second = _chunk_sums(jnp.where(ridx == 0, new_ref[:, 0:width], 0.0), w)
    f_ref[tail0:tail0 + new_rows // CMP_STRIDE, :] = first
    s_ref[tail0:tail0 + new_rows // CMP_STRIDE, :] = second

    nxt = pltpu.roll(s_ref[...], ncp - 1, 0)
    rid = lax.broadcasted_iota(jnp.int32, (ncp, width), 0)
    ssum = jnp.where(rid < n_cmp, f_ref[...] + nxt, 0.0)
    ckv = jnp.dot(ssum.astype(BF16), proj_ref[...], preferred_element_type=F32).astype(BF16)

    qs = _stack_heads(q_ref[...])
    s = _dot_nt(qs, ckv[:, 0:LANES])
    col = lax.broadcasted_iota(jnp.int32, (NSA_HEADS, ncp), 1)
    p, _, _ = _masked_softmax(s, (col * CMP_STRIDE + CMP_BLOCK - 1) <= past)
    oc_ref[...] = jnp.dot(p.astype(BF16), ckv[:, LANES:width], preferred_element_type=F32)

    imp = jnp.sum(p.reshape(NSA_KV, NSA_HPG, ncp), axis=1)
    row8 = lax.broadcasted_iota(jnp.int32, (8, ncp), 0)
    imp8 = jnp.where(row8 == 0, imp[0:1], jnp.where(row8 == 1, imp[1:2], 0.0))
    sc = _dot3(imp8, smat_ref[...])
    nl = sc.shape[1]
    blk = lax.broadcasted_iota(jnp.int32, (8, nl), 1)
    cur = past // SLC_BLOCK
    forced = (blk == 0) | (blk == cur) | (blk == cur - 1)
    future = (blk * SLC_BLOCK) > past
    sc = jnp.where(forced, jnp.inf, jnp.where(future, NEG_INF, sc))
    sc = jnp.where(blk < n_sel, sc, NEG_INF)
    k = min(SLC_TOPK, n_sel)
    sel = _topk_mask(sc, n_sel, k)
    cnt = jnp.dot(sel.astype(BF16), tri_ref[...], preferred_element_type=F32)
    slot_id = lax.broadcasted_iota(jnp.int32, (SLC_TOPK, nl), 0).astype(F32)
    blk_f = lax.broadcasted_iota(jnp.int32, (SLC_TOPK, nl), 1).astype(F32)
    for g in range(NSA_KV):
        hit = (sel[g:g + 1] > 0.5) & (cnt[g:g + 1] == slot_id)
        ids = jnp.sum(jnp.where(hit, blk_f, 0.0), axis=-1, keepdims=True)
        idx_ref[g * SLC_TOPK:(g + 1) * SLC_TOPK, :] = jnp.broadcast_to(ids, (SLC_TOPK, LANES)).astype(jnp.int32)


def _nsa_dec_cmp(q8, kvf, pool, page_table, cw, cproj):
    bd, n_pages = page_table.shape
    n_phys, page = pool.shape[0], pool.shape[1]
    past = n_pages * page
    n_chunks = (past + SLC_BLOCK) // CMP_STRIDE
    ncp = _round_up(n_chunks, LANES)
    n_sel = past // SLC_BLOCK + 1
    assert n_sel <= 2 * LANES and SLC_TOPK <= n_sel
    smat = _select_matrix(ncp, n_sel)
    nl = smat.shape[1]
    tri = jnp.asarray(np.triu(np.ones((nl, nl)), 1), BF16)
    pool3 = _pool_feature_major(pool)
    kern = functools.partial(_nsa_dec_cmp_kernel, n_pages=n_pages, page=page, past=past, n_sel=n_sel)
    const = lambda i, pt: (0, 0)
    grid_spec = pltpu.PrefetchScalarGridSpec(
        num_scalar_prefetch=1,
        grid=(bd,),
        in_specs=[pl.BlockSpec((None, NSA_HPG, LANES), lambda i, pt: (i, 0, 0)),
                  pl.BlockSpec((None, 1, NSA_KVW), lambda i, pt: (i, 0, 0)),
                  pl.BlockSpec(memory_space=pl.ANY),
                  pl.BlockSpec(cw.shape, const), pl.BlockSpec(cproj.shape, const),
                  pl.BlockSpec(smat.shape, const), pl.BlockSpec(tri.shape, const)],
        out_specs=[pl.BlockSpec((None, NSA_HEADS, LANES), lambda i, pt: (i, 0, 0)),
                   pl.BlockSpec((None, NSA_KV * SLC_TOPK, LANES), lambda i, pt: (i, 0, 0))],
        scratch_shapes=[pltpu.VMEM((2, n_pages, 2 * LANES, page), F32),
                        pltpu.VMEM((ncp, 2 * LANES), F32), pltpu.VMEM((ncp, 2 * LANES), F32),
                        pltpu.SemaphoreType.DMA((2,))])
    return pl.pallas_call(
        kern,
        grid_spec=grid_spec,
        out_shape=[jax.ShapeDtypeStruct((bd, NSA_HEADS, LANES), F32),
                   jax.ShapeDtypeStruct((bd, NSA_KV * SLC_TOPK, LANES), jnp.int32)],
        compiler_params=_cparams(("arbitrary",)),
        name="nsa_dec_cmp",
    )(page_table, q8, kvf.reshape(bd, 1, NSA_KVW), pool3, cw, cproj, smat, tri)


def _nsa_dec_sel_kernel(pt_ref, ids_ref, q_ref, new_ref, gate_ref, oc_ref, idl_ref, win_ref, pool_ref,
                        o_ref, wout_ref, sbuf, sem, *, past, wb):
    b = pl.program_id(0)
    nb = pl.num_programs(0)
    slot = b % 2
    width = 2 * LANES
    n_slots = NSA_KV * SLC_TOPK
    n_past = past // SLC_BLOCK
    page = pool_ref.shape[2]
    sub = page // SLC_BLOCK

    def copy(bb, sl, s):
        ip = jnp.minimum(ids_ref[bb, s], n_past - 1)
        pg = pt_ref[bb, ip // sub]
        return pltpu.make_async_copy(pool_ref.at[pg, pl.ds(width, width), :], sbuf.at[sl, s], sem.at[sl])

    def start(bb, sl):
        def body(s, c):
            copy(bb, sl, s).start()
            return c
        lax.fori_loop(0, n_slots, body, 0)

    @pl.when(b == 0)
    def _():
        start(0, 0)

    @pl.when(b + 1 < nb)
    def _():
        start(b + 1, 1 - slot)

    def wait_body(s, c):
        copy(b, slot, s).wait()
        return c
    lax.fori_loop(0, n_slots, wait_body, 0)

    qs = _stack_heads(q_ref[...])
    new = new_ref[...]
    nk = SLC_TOPK * page
    low = lax.broadcasted_iota(jnp.int32, (1, LANES), 1) < HEAD_DIM

    new_sb = new[:, width:2 * width].astype(BF16).astype(F32)
    ecol = jnp.where(lax.broadcasted_iota(jnp.int32, (LANES, nk), 0)
                     == _div_p2(lax.broadcasted_iota(jnp.int32, (LANES, nk), 1), page), 1.0, 0.0).astype(BF16)
    blk_in_page = _div_p2(_mod_p2(lax.broadcasted_iota(jnp.int32, (8, nk), 1), page), SLC_BLOCK)
    o_sel = []
    for g in range(NSA_KV):
        qg = qs[g * NSA_HPG:(g + 1) * NSA_HPG]
        idl = idl_ref[:, g * LANES:(g + 1) * LANES]
        blk_l = jnp.dot(idl.astype(BF16), ecol, preferred_element_type=F32).astype(jnp.int32)
        from_pool = blk_l < n_past
        valid = from_pool & (_mod_p2(blk_l, sub) == blk_in_page)
        has_new = jnp.max(jnp.where(from_pool, 0.0, 1.0), axis=-1, keepdims=True) > 0.5
        kt = jnp.concatenate([sbuf[slot, g * SLC_TOPK + k, 0:LANES, :] for k in range(SLC_TOPK)],
                             axis=1).astype(BF16)
        vt = jnp.concatenate([sbuf[slot, g * SLC_TOPK + k, LANES:width, :] for k in range(SLC_TOPK)],
                             axis=1).astype(BF16)
        s = jnp.where(valid, jnp.dot(qg, kt, preferred_element_type=F32), NEG_INF)
        s_n = jnp.sum(qg.astype(F32) * new_sb[:, 0:LANES], axis=-1, keepdims=True)
        s_n = jnp.where(has_new, s_n, NEG_INF)
        m = jnp.maximum(jnp.max(s, axis=-1, keepdims=True), s_n)
        m = jnp.where(m > NEG_INF, m, 0.0)
        e = jnp.exp(s - m)
        e_n = jnp.exp(s_n - m)
        l = jnp.maximum(jnp.sum(e, axis=-1, keepdims=True) + e_n, 1e-30)
        o_sel.append(_dot_nt((e / l).astype(BF16), vt)
                     + (e_n / l).astype(BF16).astype(F32) * new_sb[:, LANES:width])
    os_ = jnp.concatenate(o_sel, axis=0)

    win = win_ref[...]
    winb = win.astype(BF16)
    new_win = new[:, 2 * width:3 * width]
    new_wb = new_win.astype(BF16).astype(F32)
    s_w = jnp.dot(qs, winb[0:LANES], preferred_element_type=F32)
    s_n = jnp.sum(qs.astype(F32) * new_wb[:, 0:LANES], axis=-1, keepdims=True)
    kidx = lax.broadcasted_iota(jnp.int32, (NSA_HEADS, wb), 1)
    dist = wb - kidx
    s_w = jnp.where(dist <= NSA_WINDOW, s_w, NEG_INF)
    m = jnp.maximum(jnp.max(s_w, axis=-1, keepdims=True), s_n)
    e_w = jnp.exp(s_w - m)
    e_n = jnp.exp(s_n - m)
    l = jnp.maximum(jnp.sum(e_w, axis=-1, keepdims=True) + e_n, 1e-30)
    ow = (_dot_nt((e_w / l).astype(BF16), winb[LANES:width])
          + (e_n / l).astype(BF16).astype(F32) * new_wb[:, LANES:width])

    gl = lax.broadcasted_iota(jnp.int32, (NSA_HEADS, LANES), 1)
    gh = lax.broadcasted_iota(jnp.int32, (NSA_HEADS, LANES), 0)
    grow = jnp.broadcast_to(gate_ref[...], (NSA_HEADS, LANES))
    gcol = lambda br: jnp.sum(jnp.where(gl == br * NSA_HEADS + gh, grow, 0.0), axis=-1, keepdims=True)
    x = gcol(0) * oc_ref[...] + gcol(1) * os_ + gcol(2) * ow
    o_ref[...] = jnp.where(low, x[0:NSA_HPG], x[NSA_HPG:]).astype(BF16)

    _shift_rows_left(win_ref, wout_ref, new_win)


def _nsa_dec_sel(q8, kvf, gates, oc, ids, pool, win_buf, page_table):
    bd, n_pages = page_table.shape
    n_phys, page = pool.shape[0], pool.shape[1]
    past = n_pages * page
    wb = win_buf.shape[1]
    ids2 = ids[:, :, 0]
    idl = ids2.astype(F32).reshape(bd, NSA_KV, SLC_TOPK)
    idl = jnp.pad(idl, ((0, 0), (0, 0), (0, LANES - SLC_TOPK))).reshape(bd, 1, NSA_KV * LANES)
    idl = jnp.broadcast_to(idl, (bd, 8, NSA_KV * LANES))
    assert wb % LANES == 0 and page % SLC_BLOCK == 0
    pool3 = _pool_feature_major(pool)
    win3 = jnp.transpose(win_buf, (0, 2, 3, 4, 1)).reshape(bd, 2 * LANES, wb)
    kern = functools.partial(_nsa_dec_sel_kernel, past=past, wb=wb)
    row3 = lambda i, pt, sid: (i, 0, 0)
    grid_spec = pltpu.PrefetchScalarGridSpec(
        num_scalar_prefetch=2,
        grid=(bd,),
        in_specs=[pl.BlockSpec((None, NSA_HPG, LANES), row3), pl.BlockSpec((None, 1, NSA_KVW), row3),
                  pl.BlockSpec((None, 1, LANES), row3), pl.BlockSpec((None, NSA_HEADS, LANES), row3),
                  pl.BlockSpec((None, 8, NSA_KV * LANES), row3), pl.BlockSpec((None, 2 * LANES, wb), row3),
                  pl.BlockSpec(memory_space=pl.ANY)],
        out_specs=[pl.BlockSpec((None, NSA_HPG, LANES), row3), pl.BlockSpec((None, 2 * LANES, wb), row3)],
        scratch_shapes=[pltpu.VMEM((2, NSA_KV * SLC_TOPK, 2 * LANES, page), F32),
                        pltpu.SemaphoreType.DMA((2,))])
    o, wout = pl.pallas_call(
        kern,
        grid_spec=grid_spec,
        out_shape=[jax.ShapeDtypeStruct((bd, NSA_HPG, LANES), BF16),
                   jax.ShapeDtypeStruct((bd, 2 * LANES, wb), F32)],
        compiler_params=_cparams(("arbitrary",)),
        name="nsa_dec_sel",
    )(page_table, ids2, q8, kvf.reshape(bd, 1, NSA_KVW), gates.reshape(bd, 1, LANES), oc, idl, win3, pool3)
    wout = jnp.transpose(wout.reshape(bd, 2, NSA_KV, HEAD_DIM, wb), (0, 4, 1, 2, 3))
    return o, wout


def _dil_dec_kernel(hq_ref, st_ref, c0_ref, c1_ref, c2_ref, o_ref, s0_ref, s1_ref, s2_ref):
    c = pl.program_id(1)
    caches = (c0_ref, c1_ref, c2_ref)
    outs = (s0_ref, s1_ref, s2_ref)
    tiles = DIL_GW // LANES
    low = lax.broadcasted_iota(jnp.int32, (1, LANES), 1) < HEAD_DIM
    row = lax.broadcasted_iota(jnp.int32, (8, LANES), 0)
    lane8 = lax.broadcasted_iota(jnp.int32, (8, LANES), 1)
    mine = ((row == 0) & (lane8 < HEAD_DIM)) | ((row == 1) & (lane8 >= HEAD_DIM))
    o_g, lse_g = [], []
    for g, (span, r) in enumerate(DIL_PAIRS):
        wb = caches[g].shape[2]
        q = hq_ref[3 * g * tiles + c].astype(F32)
        kn = st_ref[2 * g * tiles + c]
        vn = st_ref[(2 * g + 1) * tiles + c]
        q2 = jnp.where(mine, jnp.broadcast_to(q, (8, LANES)), 0.0)
        knb = kn.astype(BF16).astype(F32)
        vnb = vn.astype(BF16).astype(F32)
        kt = caches[g][0].astype(BF16)
        vt = caches[g][1].astype(BF16)
        s = jnp.dot(q2.astype(BF16), kt, preferred_element_type=F32)
        t = lax.broadcasted_iota(jnp.int32, (8, wb), 1)
        s = jnp.where(_mod_p2(t, r) == 0, s, NEG_INF)
        s_n = jnp.sum(q2 * knb, axis=-1, keepdims=True)
        m = jnp.maximum(jnp.max(s, axis=-1, keepdims=True), s_n)
        e = jnp.exp(s - m)
        e_n = jnp.exp(s_n - m)
        l = jnp.maximum(jnp.sum(e, axis=-1, keepdims=True) + e_n, 1e-30)
        o_g.append(_dot_nt((e / l).astype(BF16), vt) + (e_n / l).astype(BF16).astype(F32) * vnb)
        lse_g.append(m + jnp.log(l))
        _shift_rows_left(caches[g].at[0], outs[g].at[0], kn)
        _shift_rows_left(caches[g].at[1], outs[g].at[1], vn)
    mx = jnp.maximum(jnp.maximum(lse_g[0], lse_g[1]), lse_g[2])
    es = [jnp.exp(x - mx) for x in lse_g]
    den = es[0] + es[1] + es[2]
    o = (es[0] / den) * o_g[0] + (es[1] / den) * o_g[1] + (es[2] / den) * o_g[2]
    o_ref[...] = jnp.where(low, o[0:1], o[1:2]).astype(BF16)


def _dil_dec(hb, st, bufs):
    bd = hb.shape[0]
    tiles = DIL_GW // LANES
    views, specs, shapes = [], [], []
    for (span, r), buf in zip(DIL_PAIRS, bufs):
        wb = buf.shape[1]
        assert wb == span and wb % r == 0 and wb % LANES == 0
        views.append(jnp.transpose(buf, (0, 2, 3, 4, 1)).reshape(bd, 2, tiles, LANES, wb))
        specs.append(pl.BlockSpec((None, 2, None, LANES, wb), lambda i, c: (i, 0, c, 0, 0)))
        shapes.append(jax.ShapeDtypeStruct((bd, 2, tiles, LANES, wb), F32))
    nq = hb.shape[1] // LANES
    ns = st.shape[1] // LANES
    res = pl.pallas_call(
        _dil_dec_kernel,
        grid=(bd, tiles),
        in_specs=[pl.BlockSpec((None, nq, 1, LANES), lambda i, c: (i, 0, 0, 0)),
                  pl.BlockSpec((None, ns, 1, LANES), lambda i, c: (i, 0, 0, 0))] + specs,
        out_specs=[pl.BlockSpec((None, None, 1, LANES), lambda i, c: (i, c, 0, 0))] + specs,
        out_shape=[jax.ShapeDtypeStruct((bd, tiles, 1, LANES), BF16)] + shapes,
        compiler_params=_cparams(("parallel", "parallel")),
        name="dil_dec",
    )(hb.reshape(bd, nq, 1, LANES), st.reshape(bd, ns, 1, LANES), *views)
    states = [jnp.transpose(s.reshape(bd, 2, DIL_HEADS, HEAD_DIM, s.shape[-1]), (0, 4, 1, 2, 3)) for s in res[1:]]
    return res[0].reshape(bd, DIL_GW), states


def _mla_qlat_kernel(q_ref, wkt_ref, ql_ref, qp_ref):
    for h in range(MLA_HEADS):
        qh = q_ref[:, h * LANES:(h + 1) * LANES]
        ql_ref[h] = jnp.dot(qh, wkt_ref[h], preferred_element_type=F32).astype(BF16)
        lane = lax.broadcasted_iota(jnp.int32, qh.shape, 1)
        qpe = jnp.where((lane >= MLA_NOPE) & (lane < MLA_NOPE + MLA_ROPE), qh.astype(F32), 0.0)
        qp_ref[h] = pltpu.roll(qpe, LANES - MLA_NOPE - MLA_ROPE, 1).astype(BF16)


def _mla_qlat(q, wkt):
    bd = q.shape[0]
    return pl.pallas_call(
        _mla_qlat_kernel,
        out_shape=[jax.ShapeDtypeStruct((MLA_HEADS, bd, MLA_KV_RANK), BF16),
                   jax.ShapeDtypeStruct((MLA_HEADS, bd, LANES), BF16)],
        compiler_params=pltpu.CompilerParams(vmem_limit_bytes=VMEM_LIMIT),
        name="mla_qlat",
    )(q, wkt)


def _mla_dec_kernel(pt_ref, ql_ref, qp_ref, new_ref, pool_ref, o_ref, buf, m_ref, l_ref, acc_ref, sem,
                    *, n_pages, page, past, chunk):
    b = pl.program_id(0)
    nb = pl.num_programs(0)
    slot = b % 2

    def fetch(bb, sl):
        def mk(pg):
            return pltpu.make_async_copy(pool_ref.at[pt_ref[bb, pg]], buf.at[sl, pg], sem.at[sl])
        return mk

    def start(bb, sl):
        mk = fetch(bb, sl)

        def body(pg, c):
            mk(pg).start()
            return c
        lax.fori_loop(0, n_pages, body, 0)

    @pl.when(b == 0)
    def _():
        start(0, 0)

    @pl.when(b + 1 < nb)
    def _():
        start(b + 1, 1 - slot)

    mk = fetch(b, slot)

    def wait_body(pg, c):
        mk(pg).wait()
        return c
    lax.fori_loop(0, n_pages, wait_body, 0)

    ql = ql_ref[...]
    qp = qp_ref[...]
    pe0 = MLA_ST - LANES
    _online_init(m_ref, l_ref, acc_ref)
    ppc = chunk // page

    def body(c, carry):
        x = jnp.concatenate([buf[slot, c * ppc + u] for u in range(ppc)], axis=1).astype(BF16)
        lat = x[0:MLA_KV_RANK]
        s = (jnp.dot(ql, lat, preferred_element_type=F32)
             + jnp.dot(qp, x[pe0:MLA_ST], preferred_element_type=F32)) * MLA_SCALE
        m_old = m_ref[...]
        m_new = jnp.maximum(m_old, jnp.max(s, axis=-1, keepdims=True))
        alpha = jnp.exp(m_old - m_new)
        p = jnp.exp(s - m_new)
        l_ref[...] = alpha * l_ref[...] + jnp.sum(p, axis=-1, keepdims=True)
        acc_ref[...] = alpha * acc_ref[...] + _dot_nt(p.astype(BF16), lat)
        m_ref[...] = m_new
        return carry
    lax.fori_loop(0, past // chunk, body, 0)

    newb = new_ref[...].astype(BF16).astype(F32)
    s_n = (jnp.sum(ql.astype(F32) * newb[:, 0:MLA_KV_RANK], axis=-1, keepdims=True)
           + jnp.sum(qp.astype(F32)[:, LANES - MLA_ROPE:] * newb[:, MLA_KV_RANK:MLA_ST], axis=-1, keepdims=True)
           ) * MLA_SCALE
    m_old = m_ref[...]
    m_new = jnp.maximum(m_old, s_n)
    alpha = jnp.exp(m_old - m_new)
    p_n = jnp.exp(s_n - m_new)
    l = jnp.maximum(alpha * l_ref[...] + p_n, 1e-30)
    acc = alpha * acc_ref[...] + p_n.astype(BF16).astype(F32) * newb[:, 0:MLA_KV_RANK]
    o_ref[...] = acc / l


def _mla_dec(ql, qp, st, pool, page_table, chunk=1024):
    bd, n_pages = page_table.shape
    page = pool.shape[1]
    past = n_pages * page
    chunk = min(chunk, past)
    kern = functools.partial(_mla_dec_kernel, n_pages=n_pages, page=page, past=past, chunk=chunk)
    row3 = lambda i, pt: (i, 0, 0)
    grid_spec = pltpu.PrefetchScalarGridSpec(
        num_scalar_prefetch=1,
        grid=(bd,),
        in_specs=[pl.BlockSpec((None, MLA_HEADS, MLA_KV_RANK), row3), pl.BlockSpec((None, MLA_HEADS, LANES), row3),
                  pl.BlockSpec((None, 1, MLA_ST), row3), pl.BlockSpec(memory_space=pl.ANY)],
        out_specs=pl.BlockSpec((None, MLA_HEADS, MLA_KV_RANK), row3),
        scratch_shapes=[pltpu.VMEM((2, n_pages, MLA_ST, page), F32), pltpu.VMEM((MLA_HEADS, 1), F32),
                        pltpu.VMEM((MLA_HEADS, 1), F32), pltpu.VMEM((MLA_HEADS, MLA_KV_RANK), F32),
                        pltpu.SemaphoreType.DMA((2,))])
    return pl.pallas_call(
        kern,
        grid_spec=grid_spec,
        out_shape=jax.ShapeDtypeStruct((bd, MLA_HEADS, MLA_KV_RANK), F32),
        compiler_params=_cparams(("arbitrary",)),
        name="mla_dec",
    )(page_table, ql, qp, st.reshape(bd, 1, MLA_ST), jnp.transpose(pool, (0, 2, 1)))


def _mla_dec_out_kernel(ol_ref, wvp_ref, w_ref, r_ref, y_ref, o_scr):
    for j in range(MLA_HEADS // 2):
        acc = jnp.dot(ol_ref[2 * j].astype(BF16), wvp_ref[2 * j], preferred_element_type=F32)
        acc = acc + jnp.dot(ol_ref[2 * j + 1].astype(BF16), wvp_ref[2 * j + 1], preferred_element_type=F32)
        o_scr[:, j * LANES:(j + 1) * LANES] = acc.astype(BF16)
    y_ref[...] = r_ref[...] + jnp.dot(o_scr[...], w_ref[...], preferred_element_type=F32)


def _mla_dec_out(ol, wvp, w_out, res):
    bd, d = res.shape
    return pl.pallas_call(
        _mla_dec_out_kernel,
        out_shape=jax.ShapeDtypeStruct((bd, d), F32),
        scratch_shapes=[pltpu.VMEM((bd, MLA_HEADS * MLA_V), BF16)],
        compiler_params=pltpu.CompilerParams(vmem_limit_bytes=VMEM_LIMIT),
        name="mla_dec_out",
    )(ol, wvp, w_out, res)


def _dil_sample_layer(h, p, bufs, past):
    bd = h.shape[0]
    tabs = _rope_tables(jnp.full((bd,), past), ROT_DIM, ROPE_THETA, HEAD_DIM, 0)
    hb, st = _dil_proj(h, p['norm_attn'], p['w_in'].astype(BF16), tabs, 1, bd)
    o, states = _dil_dec(hb, st, bufs)
    y = _out_proj(o, p['w_out'].astype(BF16), h, bd)
    return y, states


def _mla_sample_layer(h, p, pool, page_table):
    bd = h.shape[0]
    past = page_table.shape[1] * pool.shape[1]
    ws = _mla_weights(p['w_in'], p['w_uq'], p['w_ukv'])
    pos = jnp.full((bd,), past)
    qtabs = _rope_tables(pos, MLA_ROPE, MLA_THETA, LANES, MLA_NOPE)
    ktabs = _rope_tables(pos, MLA_ROPE, MLA_THETA, LANES, 0)
    q, _, _, st = _mla_proj(h, p, ws, qtabs, ktabs, 1, bd)
    ukv = p['w_ukv'].reshape(MLA_KV_RANK, MLA_HEADS, MLA_NOPE + MLA_V)
    wkt = jnp.transpose(ukv[..., :MLA_NOPE], (1, 2, 0))
    wkt = jnp.concatenate([wkt, jnp.zeros((MLA_HEADS, LANES - MLA_NOPE, MLA_KV_RANK), wkt.dtype)], axis=1)
    ql, qp = _mla_qlat(q, wkt.astype(BF16))
    ol = _mla_dec(jnp.transpose(ql, (1, 0, 2)), jnp.transpose(qp, (1, 0, 2)), st, pool, page_table)
    wv = jnp.transpose(ukv[..., MLA_NOPE:], (1, 0, 2))
    z = jnp.zeros_like(wv)
    even = (np.arange(MLA_HEADS) % 2 == 0)[:, None, None]
    wvp = jnp.concatenate([jnp.where(even, wv, z), jnp.where(even, z, wv)], axis=2)
    y = _mla_dec_out(jnp.transpose(ol, (1, 0, 2)), wvp.astype(BF16), p['w_out'].astype(BF16), h)
    return y, st.reshape(bd, 1, MLA_ST)


def _nsa_sample_layer(h, p, pool, win_buf, page_table):
    bd = h.shape[0]
    past = page_table.shape[1] * pool.shape[1]
    w_in, w_out = _nsa_weights(p['w_in'], p['w_out'])
    tabs = _rope_tables(jnp.full((bd,), past), ROT_DIM, ROPE_THETA, HEAD_DIM, 0)
    q, kvf, _, gates = _nsa_proj(h, p['norm_attn'], w_in, tabs, 1, bd)
    q8 = q.reshape(bd, NSA_HPG, LANES)
    cw, cproj = _cmp_weights(p['cmp_pos'], p['cmp_proj'])
    oc, ids = _nsa_dec_cmp(q8, kvf, pool, page_table, cw, cproj)
    o, wout = _nsa_dec_sel(q8, kvf, gates, oc, ids, pool, win_buf, page_table)
    y = _out_proj(o.reshape(bd, NSA_QW), w_out, h, bd)
    kv5 = kvf.reshape(bd, 1, 6, NSA_KV, HEAD_DIM)
    return y, kv5[:, :, 0:4], wout.reshape(win_buf.shape)


def _dil_prompt_layer(h, p, b, t, tm):
    w_in = p['w_in'].astype(BF16)
    tabs = _rope_tables(jnp.arange(t), ROT_DIM, ROPE_THETA, HEAD_DIM, 0)
    *hbs, st = _dil_proj_prompt(h, p['norm_attn'], w_in, tabs, b, t, tm)
    os, lses, states = [], [], []
    for g, (span, r) in enumerate(DIL_PAIRS):
        o, lse = _dil_attn(hbs[g], b, t, span, r)
        os.append(o)
        lses.append(lse)
        keep = min(span, t)
        s = st[:, g, :, t - keep:].reshape(b, 2, DIL_HEADS, HEAD_DIM, keep)
        states.append(jnp.transpose(s, (0, 4, 1, 2, 3)))
    y = _dil_out(os, lses, p['w_out'].astype(BF16), h, tm)
    return y, states


def _mla_prompt_layer(h, p, b, t, tm):
    ws = _mla_weights(p['w_in'], p['w_uq'], p['w_ukv'])
    pos = jnp.arange(t)
    qtabs = _rope_tables(pos, MLA_ROPE, MLA_THETA, LANES, MLA_NOPE)
    ktabs = _rope_tables(pos, MLA_ROPE, MLA_THETA, LANES, 0)
    q, k, v, st = _mla_proj(h, p, ws, qtabs, ktabs, t // tm, tm, prompt_batch=b)
    o = _mla_attn(q, k, v, b, t)
    y = _out_proj(o.reshape(b * t, MLA_HEADS * MLA_V), p['w_out'].astype(BF16), h, tm)
    return y, jnp.transpose(st, (0, 2, 1))


def _nsa_prompt_layer(h, p, b, t, tm):
    w_in, w_out = _nsa_weights(p['w_in'], p['w_out'])
    tabs = _rope_tables(jnp.arange(t), ROT_DIM, ROPE_THETA, HEAD_DIM, 0)
    cw, cproj = _cmp_weights(p['cmp_pos'], p['cmp_proj'])
    q, kvb, gates, nsa_t, win_t, fs = _nsa_proj_prompt(h, p['norm_attn'], w_in, tabs, cw, b, t, tm)
    ckv = _cmp_prompt(fs, cproj, b, t)
    o = _nsa_attn2(q, kvb, ckv, gates, b, t)
    y = _out_proj(o.reshape(b * t, NSA_QW), w_out, h, tm)
    nsa_p = jnp.transpose(nsa_t.reshape(b, 4, NSA_KV, HEAD_DIM, t), (0, 4, 1, 2, 3))
    wk = min(NSA_WINDOW, t)
    win_p = jnp.transpose(win_t[:, :, t - wk:].reshape(b, 2, NSA_KV, HEAD_DIM, wk), (0, 4, 1, 2, 3))
    return y, nsa_p, win_p


def kernel(x_prompt, x_sample, cache_l0_nsa, cache_l0_win, cache_l1_dil_w128, cache_l1_dil_w512, cache_l1_dil_w2048, cache_l2_mla, cache_l3_nsa, cache_l3_win, page_table, l0_norm_attn, l0_w_in, l0_cmp_pos, l0_cmp_proj, l0_w_out, l0_norm_mlp, l0_w_up, l0_w_down, l1_norm_attn, l1_w_in, l1_w_out, l1_norm_mlp, l1_w_up, l1_w_down, l2_norm_attn, l2_w_in, l2_q_norm, l2_w_uq, l2_kv_norm, l2_w_ukv, l2_w_out, l2_norm_mlp, l2_w_up, l2_w_down, l3_norm_attn, l3_w_in, l3_cmp_pos, l3_cmp_proj, l3_w_out, l3_norm_mlp, l3_w_up, l3_w_down, final_norm):
    layers = [
        dict(norm_attn=l0_norm_attn, w_in=l0_w_in, cmp_pos=l0_cmp_pos, cmp_proj=l0_cmp_proj, w_out=l0_w_out,
             norm_mlp=l0_norm_mlp, w_up=l0_w_up, w_down=l0_w_down),
        dict(norm_attn=l1_norm_attn, w_in=l1_w_in, w_out=l1_w_out,
             norm_mlp=l1_norm_mlp, w_up=l1_w_up, w_down=l1_w_down),
        dict(norm_attn=l2_norm_attn, w_in=l2_w_in, q_norm=l2_q_norm, w_uq=l2_w_uq, kv_norm=l2_kv_norm,
             w_ukv=l2_w_ukv, w_out=l2_w_out, norm_mlp=l2_norm_mlp, w_up=l2_w_up, w_down=l2_w_down),
        dict(norm_attn=l3_norm_attn, w_in=l3_w_in, cmp_pos=l3_cmp_pos, cmp_proj=l3_cmp_proj, w_out=l3_w_out,
             norm_mlp=l3_norm_mlp, w_up=l3_w_up, w_down=l3_w_down),
    ]
    caches = [
        dict(nsa=cache_l0_nsa, win=cache_l0_win),
        dict(dil=(cache_l1_dil_w128, cache_l1_dil_w512, cache_l1_dil_w2048)),
        dict(mla=cache_l2_mla),
        dict(nsa=cache_l3_nsa, win=cache_l3_win),
    ]
    b, t, d = x_prompt.shape
    bd, s, _ = x_sample.shape
    assert s == 1, "decode kernels handle one new token per sequence"
    past = page_table.shape[1] * cache_l0_nsa.shape[1]
    tm = _row_tile(t, 512)
    h_p = x_prompt.reshape(b * t, d)
    h_s = x_sample.reshape(bd, d)
    state = []
    for i, (p, c) in enumerate(zip(layers, caches)):
        kind = i % 3
        if kind == 0:
            h_p, nsa_p, win_p = _nsa_prompt_layer(h_p, p, b, t, tm)
            h_s, nsa_s, win_s = _nsa_sample_layer(h_s, p, c['nsa'], c['win'], page_table)
            state += [nsa_p, nsa_s, win_p, win_s]
        elif kind == 1:
            h_p, st_p = _dil_prompt_layer(h_p, p, b, t, tm)
            h_s, st_s = _dil_sample_layer(h_s, p, c['dil'], past)
            for sp, ss in zip(st_p, st_s):
                state += [sp, ss]
        else:
            h_p, mla_p = _mla_prompt_layer(h_p, p, b, t, tm)
            h_s, mla_s = _mla_sample_layer(h_s, p, c['mla'], page_table)
            state += [mla_p, mla_s]
        fin = final_norm if i == len(layers) - 1 else None
        w_up, w_down = p['w_up'].astype(BF16), p['w_down'].astype(BF16)
        h_p = _mlp(h_p, p['norm_mlp'], w_up, w_down, fin, tm)
        h_s = _mlp(h_s, p['norm_mlp'], w_up, w_down, fin, bd)
    return (h_p.reshape(b, t, d), h_s.reshape(bd, s, d), *state)
```
